```python
import math
import jax, jax.numpy as jnp
from jax import lax
import numpy as np

D_MODEL = 2048
BATCH = 4
SEQ = 2048
DEPTH = 1
DEC_BATCH = 128
DEC_SEQ = 4
PAST_LEN = 16384
PAGE_SIZE = 128

SSM_WIDTH = D_MODEL // 2
SSM_HEADDIM = 64
SSM_HEADS = SSM_WIDTH // SSM_HEADDIM
SSM_GROUPS = 2
SSM_REP = SSM_HEADS // SSM_GROUPS
D_STATE = 128
CONV_WIDTH = 4
CONV_DIM = SSM_WIDTH + 2 * SSM_GROUPS * D_STATE
SSD_CHUNK = 128
ATT_WIDTH = D_MODEL - SSM_WIDTH
HEAD_DIM = 64
ATT_HEADS = ATT_WIDTH // HEAD_DIM
KV_HEADS = 2
KV_REP = ATT_HEADS // KV_HEADS
WINDOW = 128
ROT_DIM = HEAD_DIM // 4
ROPE_THETA = 500000.0
MIX_WIDTH = SSM_WIDTH + ATT_WIDTH
IN_DIM = SSM_WIDTH + CONV_DIM + SSM_HEADS + ATT_WIDTH + 2 * KV_HEADS * HEAD_DIM
N_MEM = 256
X_HEADS = 4
X_HEAD_DIM = 128
X_WIDTH = X_HEADS * X_HEAD_DIM
D_FF = -(-8 * D_MODEL // (3 * 256)) * 256
EPS = 1e-5

kernel_name = 'hymba_ssd_swa_memxattn_step'


def rmsnorm(x, w):
    xf = x.astype(jnp.float32)
    y = xf * lax.rsqrt(jnp.mean(xf * xf, axis=-1, keepdims=True) + EPS)
    return (y * w.astype(jnp.float32)).astype(x.dtype)


def rope(x, pos):
    half = ROT_DIM // 2
    inv = ROPE_THETA ** (-jnp.arange(half, dtype=jnp.float32) * (2.0 / ROT_DIM))
    ang = pos.astype(jnp.float32)[:, None] * inv[None, :]
    cos = jnp.cos(ang)[None, :, None, :]
    sin = jnp.sin(ang)[None, :, None, :]
    xr = x[..., :ROT_DIM].astype(jnp.float32)
    x1, x2 = xr[..., :half], xr[..., half:]
    rot = jnp.concatenate([x1 * cos - x2 * sin, x2 * cos + x1 * sin], axis=-1).astype(x.dtype)
    return jnp.concatenate([rot, x[..., ROT_DIM:]], axis=-1)


def causal_conv(xbc, buf, w, bias):
    T = xbc.shape[1]
    xp = jnp.concatenate([buf.astype(xbc.dtype), xbc], axis=1)
    out = bias
    for k in range(CONV_WIDTH):
        out = out + xp[:, k:k + T] * w[k]
    return jax.nn.silu(out), xp[:, -(CONV_WIDTH - 1):]


def ssd_scan(x, dt, a, bmat, cmat, h0):
    b, T, G, R, P = x.shape
    N = bmat.shape[-1]
    cl = SSD_CHUNK if T % SSD_CHUNK == 0 else T
    nc = T // cl
    f32 = jnp.float32
    xr = x.astype(f32).reshape(b, nc, cl, G, R, P)
    dtr = dt.astype(f32).reshape(b, nc, cl, G, R)
    br = bmat.astype(f32).reshape(b, nc, cl, G, N)
    cr = cmat.astype(f32).reshape(b, nc, cl, G, N)
    cs = jnp.cumsum(dtr * a.astype(f32), axis=2)
    causal = jnp.tril(jnp.ones((cl, cl), bool))[None, None, :, :, None, None]
    seg = cs[:, :, :, None] - cs[:, :, None, :]
    decay = jnp.exp(jnp.where(causal, seg, -jnp.inf))
    cb = jnp.einsum('bctgn,bcsgn->bctsg', cr, br)
    wts = cb[..., None] * decay * dtr[:, :, None]
    y_diag = jnp.einsum('bctsgr,bcsgrp->bctgrp', wts, xr)
    to_end = jnp.exp(cs[:, :, -1:] - cs) * dtr
    chunk_states = jnp.einsum('bclgn,bclgrp->bcgrpn', br, xr * to_end[..., None])
    chunk_decay = jnp.exp(cs[:, :, -1])

    def step(h, inp):
        st, dec = inp
        return h * dec[..., None, None] + st, h

    h_last, h_starts = lax.scan(step, h0.astype(f32),
                                (jnp.moveaxis(chunk_states, 1, 0), jnp.moveaxis(chunk_decay, 1, 0)))
    h_starts = jnp.moveaxis(h_starts, 0, 1)
    y_off = jnp.einsum('bclgn,bcgrpn->bclgrp', cr, h_starts) * jnp.exp(cs)[..., None]
    y = (y_diag + y_off).reshape(b, T, G, R, P)
    return y, h_last


def sink_softmax(s, mask, sink):
    s = jnp.where(mask, s, -jnp.inf)
    m = jnp.maximum(jnp.max(s, axis=-1, keepdims=True), sink)
    e = jnp.exp(s - m)
    return e / (jnp.sum(e, axis=-1, keepdims=True) + jnp.exp(sink - m))


def swa_banded(q, k, v, sinks):
    b, T = q.shape[0], q.shape[1]
    nb = T // WINDOW
    qb = q.reshape(b, nb, WINDOW, KV_HEADS, KV_REP, HEAD_DIM)
    pad = jnp.zeros((b, WINDOW, KV_HEADS, HEAD_DIM), k.dtype)
    kp = jnp.concatenate([pad, k], axis=1).reshape(b, nb + 1, WINDOW, KV_HEADS, HEAD_DIM)
    vp = jnp.concatenate([pad, v], axis=1).reshape(b, nb + 1, WINDOW, KV_HEADS, HEAD_DIM)
    kb = jnp.concatenate([kp[:, :-1], kp[:, 1:]], axis=2)
    vb = jnp.concatenate([vp[:, :-1], vp[:, 1:]], axis=2)
    s = jnp.einsum('bnqgrd,bnkgd->bngrqk', qb, kb).astype(jnp.float32) * HEAD_DIM ** -0.5
    qi = jnp.arange(WINDOW)[:, None]
    kj = jnp.arange(2 * WINDOW)[None, :]
    rel = qi + WINDOW - kj
    blk = jnp.arange(nb)[:, None, None]
    mask = (rel >= 0) & (rel < WINDOW) & (blk * WINDOW - WINDOW + kj >= 0)
    p = sink_softmax(s, mask[None, :, None, None], sinks)
    o = jnp.einsum('bngrqk,bnkgd->bnqgrd', p.astype(v.dtype), vb)
    return o.reshape(b, T, ATT_HEADS, HEAD_DIM)


def swa_cached(q, kcat, vcat, q_pos, k_pos, sinks):
    b, T = q.shape[0], q.shape[1]
    qg = q.reshape(b, T, KV_HEADS, KV_REP, HEAD_DIM)
    s = jnp.einsum('btgrd,bkgd->bgrtk', qg, kcat).astype(jnp.float32) * HEAD_DIM ** -0.5
    rel = q_pos[:, None] - k_pos[None, :]
    mask = (rel >= 0) & (rel < WINDOW)
    p = sink_softmax(s, mask, sinks)
    o = jnp.einsum('bgrtk,bkgd->btgrd', p.astype(vcat.dtype), vcat)
    return o.reshape(b, T, ATT_HEADS, HEAD_DIM)


def mixer(h, pos, ssm0, conv0, kbuf, vbuf, lp):
    b, T, _ = h.shape
    s1 = SSM_WIDTH
    s2 = s1 + CONV_DIM
    s3 = s2 + SSM_HEADS
    s4 = s3 + ATT_WIDTH
    s5 = s4 + KV_HEADS * HEAD_DIM
    proj = h @ lp['w_in']
    z, xbc, dt_raw, q, k, v = jnp.split(proj, [s1, s2, s3, s4, s5], axis=-1)
    xbc, conv_new = causal_conv(xbc, conv0, lp['conv_w'], lp['conv_b'])
    xs, bm, cm = jnp.split(xbc, [SSM_WIDTH, SSM_WIDTH + SSM_GROUPS * D_STATE], axis=-1)
    xs = xs.reshape(b, T, SSM_GROUPS, SSM_REP, SSM_HEADDIM)
    dt = jax.nn.softplus((dt_raw + lp['dt_bias']).astype(jnp.float32)).reshape(b, T, SSM_GROUPS, SSM_REP)
    a = -jnp.exp(lp['a_log'].astype(jnp.float32)).reshape(SSM_GROUPS, SSM_REP)
    y, ssm_new = ssd_scan(xs, dt, a,
                          bm.reshape(b, T, SSM_GROUPS, D_STATE), cm.reshape(b, T, SSM_GROUPS, D_STATE),
                          ssm0.reshape(b, SSM_GROUPS, SSM_REP, SSM_HEADDIM, D_STATE))
    y = y + xs.astype(jnp.float32) * lp['d_skip'].astype(jnp.float32).reshape(SSM_GROUPS, SSM_REP)[:, :, None]
    y = y.astype(h.dtype).reshape(b, T, SSM_WIDTH)
    g = (y * jax.nn.silu(z)).reshape(b, T, SSM_GROUPS, SSM_WIDTH // SSM_GROUPS)
    y = rmsnorm(g, lp['gate_norm'].reshape(SSM_GROUPS, -1)).reshape(b, T, SSM_WIDTH)
    q = rope(q.reshape(b, T, ATT_HEADS, HEAD_DIM), pos)
    k = rope(k.reshape(b, T, KV_HEADS, HEAD_DIM), pos)
    v = v.reshape(b, T, KV_HEADS, HEAD_DIM)
    sinks = lp['sinks'].astype(jnp.float32).reshape(KV_HEADS, KV_REP, 1, 1)
    if kbuf is None:
        att = swa_banded(q, k, v, sinks)
        k_new, v_new = k[:, -WINDOW:], v[:, -WINDOW:]
    else:
        nbuf = kbuf.shape[1]
        kcat = jnp.concatenate([kbuf.astype(k.dtype), k], axis=1)
        vcat = jnp.concatenate([vbuf.astype(v.dtype), v], axis=1)
        k_pos = pos[0] - nbuf + jnp.arange(nbuf + T)
        att = swa_cached(q, kcat, vcat, pos, k_pos, sinks)
        k_new, v_new = kcat[:, -nbuf:], vcat[:, -nbuf:]
    out = jnp.concatenate([y, att.reshape(b, T, ATT_WIDTH)], axis=-1) @ lp['w_out']
    ssm_new = ssm_new.reshape(b, SSM_HEADS, SSM_HEADDIM, D_STATE).astype(ssm0.dtype)
    return out, ssm_new, conv_new, k_new, v_new


def memory_kv(mem, lp):
    m = rmsnorm(mem, lp['norm_mem'])
    b, M, _ = m.shape
    mk = (m @ lp['w_xk']).reshape(b, M, X_HEADS, X_HEAD_DIM)
    mv = (m @ lp['w_xv']).reshape(b, M, X_HEADS, X_HEAD_DIM)
    return mk, mv


def cross_attn(h, mk, mv, lp):
    b, T, _ = h.shape
    q = (h @ lp['w_xq']).reshape(b, T, X_HEADS, X_HEAD_DIM)
    s = jnp.einsum('bthd,bmhd->bhtm', q, mk.astype(q.dtype)).astype(jnp.float32) * X_HEAD_DIM ** -0.5
    p = jax.nn.softmax(s, axis=-1).astype(h.dtype)
    o = jnp.einsum('bhtm,bmhd->bthd', p, mv.astype(h.dtype)).reshape(b, T, X_WIDTH)
    return o @ lp['w_xo']


def ffn(h, lp):
    return (jax.nn.silu(h @ lp['w_gate']) * (h @ lp['w_up'])) @ lp['w_down']


def layer(x, pos, ssm0, conv0, kbuf, vbuf, mk, mv, lp):
    mix, ssm_new, conv_new, k_new, v_new = mixer(rmsnorm(x, lp['norm_mix']), pos, ssm0, conv0, kbuf, vbuf, lp)
    x = x + mix
    x = x + cross_attn(rmsnorm(x, lp['norm_x']), mk, mv, lp)
    x = x + ffn(rmsnorm(x, lp['norm_ffn']), lp)
    return x, ssm_new, conv_new, k_new, v_new


def setup_inputs(seed: int = 0) -> dict:
    key = jax.random.key(seed)
    ks = jax.random.split(key, 30)
    L = DEPTH
    n_buf = min(WINDOW, PAST_LEN)

    def nrm(k, shape, scale):
        return jax.random.normal(k, shape, jnp.float32) * scale

    def gain(k, shape):
        return 1.0 + 0.05 * jax.random.normal(k, shape, jnp.float32)

    dt0 = jnp.exp(jax.random.uniform(ks[13], (L, SSM_HEADS), jnp.float32,
                                     minval=math.log(1e-3), maxval=math.log(1e-1)))
    dt_bias = dt0 + jnp.log(-jnp.expm1(-dt0))
    a_log = jnp.log(jax.random.uniform(ks[14], (L, SSM_HEADS), jnp.float32, minval=1.0, maxval=16.0))
    return {
        'x_prompt': nrm(ks[0], (BATCH, SEQ, D_MODEL), 1.0),
        'x_sample': nrm(ks[1], (DEC_BATCH, DEC_SEQ, D_MODEL), 1.0),
        'mem_prompt': nrm(ks[2], (BATCH, N_MEM, D_MODEL), 1.0),
        'state_ssm': nrm(ks[3], (L, DEC_BATCH, SSM_HEADS, SSM_HEADDIM, D_STATE), 0.5),
        'state_conv': nrm(ks[4], (L, DEC_BATCH, CONV_WIDTH - 1, CONV_DIM), 1.0),
        'cache_swa_k': nrm(ks[5], (L, DEC_BATCH, n_buf, KV_HEADS, HEAD_DIM), 1.0),
        'cache_swa_v': nrm(ks[6], (L, DEC_BATCH, n_buf, KV_HEADS, HEAD_DIM), 1.0),
        'cache_mem_k': nrm(ks[7], (L, DEC_BATCH, N_MEM, X_HEADS, X_HEAD_DIM), 1.0),
        'cache_mem_v': nrm(ks[8], (L, DEC_BATCH, N_MEM, X_HEADS, X_HEAD_DIM), 1.0),
        'norm_mix': gain(ks[9], (L, D_MODEL)),
        'w_in': nrm(ks[10], (L, D_MODEL, IN_DIM), D_MODEL ** -0.5),
        'conv_w': nrm(ks[11], (L, CONV_WIDTH, CONV_DIM), CONV_WIDTH ** -0.5),
        'conv_b': nrm(ks[12], (L, CONV_DIM), 0.02),
        'dt_bias': dt_bias,
        'a_log': a_log,
        'd_skip': gain(ks[15], (L, SSM_HEADS)),
        'gate_norm': gain(ks[16], (L, SSM_WIDTH)),
        'sinks': nrm(ks[17], (L, ATT_HEADS), 0.5),
        'w_out': nrm(ks[18], (L, MIX_WIDTH, D_MODEL), MIX_WIDTH ** -0.5),
        'norm_mem': gain(ks[19], (L, D_MODEL)),
        'norm_x': gain(ks[20], (L, D_MODEL)),
        'w_xq': nrm(ks[21], (L, D_MODEL, X_WIDTH), D_MODEL ** -0.5),
        'w_xk': nrm(ks[22], (L, D_MODEL, X_WIDTH), D_MODEL ** -0.5),
        'w_xv': nrm(ks[23], (L, D_MODEL, X_WIDTH), D_MODEL ** -0.5),
        'w_xo': nrm(ks[24], (L, X_WIDTH, D_MODEL), X_WIDTH ** -0.5),
        'norm_ffn': gain(ks[25], (L, D_MODEL)),
        'w_gate': nrm(ks[26], (L, D_MODEL, D_FF), D_MODEL ** -0.5),
        'w_up': nrm(ks[27], (L, D_MODEL, D_FF), D_MODEL ** -0.5),
        'w_down': nrm(ks[28], (L, D_FF, D_MODEL), D_FF ** -0.5),
        'norm_final': gain(ks[29], (D_MODEL,)),
    }


def reference(x_prompt, x_sample, mem_prompt, state_ssm, state_conv, cache_swa_k, cache_swa_v,
              cache_mem_k, cache_mem_v, norm_mix, w_in, conv_w, conv_b, dt_bias, a_log, d_skip,
              gate_norm, sinks, w_out, norm_mem, norm_x, w_xq, w_xk, w_xv, w_xo, norm_ffn,
              w_gate, w_up, w_down, norm_final):
    bp, tp = x_prompt.shape[0], x_prompt.shape[1]
    pos_p = jnp.arange(tp)
    pos_s = PAST_LEN + jnp.arange(x_sample.shape[1])
    xp, xs = x_prompt, x_sample
    p_ssm, p_conv, p_k, p_v, p_mk, p_mv = [], [], [], [], [], []
    s_ssm, s_conv, s_k, s_v = [], [], [], []
    for l in range(DEPTH):
        lp = {'norm_mix': norm_mix[l], 'w_in': w_in[l], 'conv_w': conv_w[l], 'conv_b': conv_b[l],
              'dt_bias': dt_bias[l], 'a_log': a_log[l], 'd_skip': d_skip[l], 'gate_norm': gate_norm[l],
              'sinks': sinks[l], 'w_out': w_out[l], 'norm_mem': norm_mem[l], 'norm_x': norm_x[l],
              'w_xq': w_xq[l], 'w_xk': w_xk[l], 'w_xv': w_xv[l], 'w_xo': w_xo[l],
              'norm_ffn': norm_ffn[l], 'w_gate': w_gate[l], 'w_up': w_up[l], 'w_down': w_down[l]}
        mk_p, mv_p = memory_kv(mem_prompt, lp)
        ssm0 = jnp.zeros((bp, SSM_HEADS, SSM_HEADDIM, D_STATE), x_prompt.dtype)
        conv0 = jnp.zeros((bp, CONV_WIDTH - 1, CONV_DIM), x_prompt.dtype)
        xp, a1, a2, a3, a4 = layer(xp, pos_p, ssm0, conv0, None, None, mk_p, mv_p, lp)
        p_ssm.append(a1); p_conv.append(a2); p_k.append(a3); p_v.append(a4)
        p_mk.append(mk_p); p_mv.append(mv_p)
        xs, b1, b2, b3, b4 = layer(xs, pos_s, state_ssm[l], state_conv[l], cache_swa_k[l], cache_swa_v[l],
                                   cache_mem_k[l], cache_mem_v[l], lp)
        s_ssm.append(b1); s_conv.append(b2); s_k.append(b3); s_v.append(b4)
    y_prompt = rmsnorm(xp, norm_final)
    y_sample = rmsnorm(xs, norm_final)
    return (y_prompt, y_sample,
            jnp.stack(p_ssm), jnp.stack(p_conv), jnp.stack(p_k), jnp.stack(p_v),
            jnp.stack(p_mk), jnp.stack(p_mv),
            jnp.stack(s_ssm), jnp.stack(s_conv), jnp.stack(s_k), jnp.stack(s_v))
```

```python
import functools

import jax
import jax.numpy as jnp
from jax import lax
from jax.experimental import pallas as pl
from jax.experimental.pallas import tpu as pltpu

F32 = jnp.float32
BF16 = jnp.bfloat16

D_MODEL = 2048
SSM_WIDTH = 1024
SSM_HEADDIM = 64
SSM_HEADS = 16
SSM_GROUPS = 2
D_STATE = 128
CONV_WIDTH = 4
CONV_DIM = SSM_WIDTH + 2 * SSM_GROUPS * D_STATE
BC_WIDTH = 2 * SSM_GROUPS * D_STATE
ATT_WIDTH = 1024
HEAD_DIM = 64
ATT_HEADS = 16
KV_HEADS = 2
KV_WIDTH = KV_HEADS * HEAD_DIM
WINDOW = 128
CHUNK = 128
ROT_DIM = 16
ROPE_THETA = 500000.0
PAST_LEN = 16384
N_MEM = 256
X_HEADS = 4
X_HEAD_DIM = 128
X_WIDTH = X_HEADS * X_HEAD_DIM
EPS = 1e-5

LANES = 128
SUBLANES = 8
VMEM_LIMIT_BYTES = 56 * 1024 * 1024

PROJ_WIDTH = 4096
COL_Z = 0
COL_X = 1024
COL_Q = 2048
COL_BC = 3072
COL_KV = 3584
COL_DT = 3840

NEG_INF = float("-inf")


def _params(*semantics):
    return pltpu.CompilerParams(dimension_semantics=semantics,
                                vmem_limit_bytes=VMEM_LIMIT_BYTES)


def _rmsnorm(x, gain):
    ms = jnp.mean(x * x, axis=-1, keepdims=True)
    return x * lax.rsqrt(ms + EPS) * gain


def _silu(x):
    return x * jax.nn.sigmoid(x)


def _softplus(x):
    return jnp.maximum(x, 0.0) + jnp.log1p(jnp.exp(-jnp.abs(x)))


def _split_bf16(v, parts):
    out = []
    rem = v
    for _ in range(parts - 1):
        p = rem.astype(BF16)
        out.append(p)
        rem = rem - p.astype(F32)
    out.append(rem.astype(BF16))
    return out


def _dot(a, b):
    return jnp.dot(a, b, preferred_element_type=F32)


def _dot_nt(a, b):
    return lax.dot_general(a, b, (((1,), (1,)), ((), ())), preferred_element_type=F32)


def _dot_tn(a, b):
    return lax.dot_general(a, b, (((0,), (0,)), ((), ())), preferred_element_type=F32)


def _select_matmul(v, sel, parts):
    acc = None
    for p in _split_bf16(v, parts):
        t = _dot(p, sel)
        acc = t if acc is None else acc + t
    return acc


def _norm_matmul_kernel(x_ref, g_ref, w_ref, o_ref, h_ref):
    @pl.when(pl.program_id(1) == 0)
    def _():
        h_ref[...] = _rmsnorm(x_ref[...], g_ref[...]).astype(BF16)

    o_ref[...] = _dot(h_ref[...], w_ref[...]).astype(o_ref.dtype)


def norm_matmul(x, gain, w, *, tm, tn, out_dtype=F32):
    m, k = x.shape
    n = w.shape[1]
    return pl.pallas_call(
        _norm_matmul_kernel,
        grid=(m // tm, n // tn),
        in_specs=[pl.BlockSpec((tm, k), lambda i, j: (i, 0)),
                  pl.BlockSpec((1, k), lambda i, j: (0, 0)),
                  pl.BlockSpec((k, tn), lambda i, j: (0, j))],
        out_specs=pl.BlockSpec((tm, tn), lambda i, j: (i, j)),
        out_shape=jax.ShapeDtypeStruct((m, n), out_dtype),
        scratch_shapes=[pltpu.VMEM((tm, k), BF16)],
        compiler_params=_params("parallel", "arbitrary"),
        name="norm_matmul",
    )(x, gain, w)


def _ssd_prompt_kernel(z_ref, x_ref, bc_ref, dt_ref, cw_ref, cb_ref, dtb_ref, alog_ref,
                       dskip_ref, gn_ref, tril_ref, exp_ref,
                       y_ref, state_ref, xpad_ref, st_ref):
    c = pl.program_id(1)
    nc = pl.num_programs(1)
    t = CHUNK

    @pl.when(c == 0)
    def _():
        xpad_ref[0:SUBLANES, :] = jnp.zeros((SUBLANES, CONV_DIM), F32)
        st_ref[...] = jnp.zeros_like(st_ref)

    xpad_ref[SUBLANES:SUBLANES + t, 0:SSM_WIDTH] = x_ref[...]
    xpad_ref[SUBLANES:SUBLANES + t, SSM_WIDTH:CONV_DIM] = bc_ref[...]
    acc = jnp.broadcast_to(cb_ref[...], (t, CONV_DIM))
    for k in range(CONV_WIDTH):
        lo = SUBLANES - (CONV_WIDTH - 1) + k
        acc = acc + xpad_ref[lo:lo + t, :] * cw_ref[k:k + 1, :]
    xbc = _silu(acc)
    xpad_ref[0:SUBLANES, :] = xpad_ref[t:t + SUBLANES, :]

    xs = xbc[:, 0:SSM_WIDTH]
    xs_b = xs.astype(BF16)
    b_mat = [xbc[:, SSM_WIDTH + g * D_STATE:SSM_WIDTH + (g + 1) * D_STATE].astype(BF16)
             for g in range(SSM_GROUPS)]
    c_off = SSM_WIDTH + SSM_GROUPS * D_STATE
    c_mat = [xbc[:, c_off + g * D_STATE:c_off + (g + 1) * D_STATE].astype(BF16)
             for g in range(SSM_GROUPS)]

    dt = _softplus(dt_ref[...] + dtb_ref[...])
    a = -jnp.exp(alog_ref[...])
    da = dt * a
    cs = _select_matmul_left(tril_ref[...], da)
    cs_t = cs.T
    dt_t = dt.T
    cs_last = cs[t - 1:t, :]
    exp_cs = jnp.exp(cs)
    to_end = jnp.exp(cs_last - cs) * dt
    sel = exp_ref[...]
    exp_cs_e = _select_matmul(exp_cs, sel, 2)
    to_end_e = _select_matmul(to_end, sel, 2)

    row = lax.broadcasted_iota(jnp.int32, (t, 2 * t), 0)
    col = lax.broadcasted_iota(jnp.int32, (t, 2 * t), 1)
    causal = jnp.bitwise_and(col, t - 1) <= row
    lane = lax.broadcasted_iota(jnp.int32, (t, LANES), 1)
    left = lane < SSM_HEADDIM

    cb = [_dot_nt(c_mat[g], b_mat[g]) for g in range(SSM_GROUPS)]
    cb2 = [jnp.concatenate([m, m], axis=1) for m in cb]

    y_parts = []
    for j in range(SSM_HEADS // 2):
        g = (2 * j) // (SSM_HEADS // SSM_GROUPS)
        h0, h1 = 2 * j, 2 * j + 1
        colv = jnp.concatenate([jnp.broadcast_to(cs[:, h0:h0 + 1], (t, t)),
                                jnp.broadcast_to(cs[:, h1:h1 + 1], (t, t))], axis=1)
        rowv = jnp.concatenate([jnp.broadcast_to(cs_t[h0:h0 + 1, :], (t, t)),
                                jnp.broadcast_to(cs_t[h1:h1 + 1, :], (t, t))], axis=1)
        dtr = jnp.concatenate([jnp.broadcast_to(dt_t[h0:h0 + 1, :], (t, t)),
                               jnp.broadcast_to(dt_t[h1:h1 + 1, :], (t, t))], axis=1)
        decay = jnp.where(causal, jnp.exp(colv - rowv), 0.0)
        wts = (cb2[g] * decay * dtr).astype(BF16)
        xp = xs_b[:, j * LANES:(j + 1) * LANES]
        zero = jnp.zeros_like(xp)
        rhs = jnp.concatenate([jnp.where(left, xp, zero), jnp.where(left, zero, xp)], axis=0)
        y_parts.append(_dot(wts, rhs))
    y_diag = jnp.concatenate(y_parts, axis=1)

    half = SSM_WIDTH // SSM_GROUPS
    st = st_ref[...]
    st_b = st.astype(BF16)
    y_off = jnp.concatenate(
        [_dot(c_mat[g], st_b[:, g * half:(g + 1) * half]) for g in range(SSM_GROUPS)], axis=1)
    xw = (xs * to_end_e).astype(BF16)
    upd = jnp.concatenate(
        [_dot_tn(b_mat[g], xw[:, g * half:(g + 1) * half]) for g in range(SSM_GROUPS)], axis=1)
    st_new = st * exp_cs_e[t - 1:t, :] + upd
    st_ref[...] = st_new

    y = y_diag + y_off * exp_cs_e + xs * dskip_ref[...]
    gated = y * _silu(z_ref[...])
    outs = []
    for g in range(SSM_GROUPS):
        outs.append(_rmsnorm(gated[:, g * half:(g + 1) * half], gn_ref[:, g * half:(g + 1) * half]))
    y_ref[...] = jnp.concatenate(outs, axis=1).astype(y_ref.dtype)

    @pl.when(c == nc - 1)
    def _():
        state_ref[0] = st_new.T


def _select_matmul_left(sel, v):
    acc = None
    for p in _split_bf16(v, 3):
        t = _dot(sel, p)
        acc = t if acc is None else acc + t
    return acc


def ssd_prompt(proj, conv_w, conv_b, dtb, alog, dskip_e, gate_norm, tril, expand, *, batch, seq):
    nc = seq // CHUNK
    rows = lambda b, c: b * nc + c
    const = lambda b, c: (0, 0)
    return pl.pallas_call(
        _ssd_prompt_kernel,
        grid=(batch, nc),
        in_specs=[
            pl.BlockSpec((CHUNK, SSM_WIDTH), lambda b, c: (rows(b, c), COL_Z // SSM_WIDTH)),
            pl.BlockSpec((CHUNK, SSM_WIDTH), lambda b, c: (rows(b, c), COL_X // SSM_WIDTH)),
            pl.BlockSpec((CHUNK, BC_WIDTH), lambda b, c: (rows(b, c), COL_BC // BC_WIDTH)),
            pl.BlockSpec((CHUNK, LANES), lambda b, c: (rows(b, c), COL_DT // LANES)),
            pl.BlockSpec((CONV_WIDTH, CONV_DIM), const),
            pl.BlockSpec((1, CONV_DIM), const),
            pl.BlockSpec((1, LANES), const),
            pl.BlockSpec((1, LANES), const),
            pl.BlockSpec((1, SSM_WIDTH), const),
            pl.BlockSpec((1, SSM_WIDTH), const),
            pl.BlockSpec((CHUNK, CHUNK), const),
            pl.BlockSpec((LANES, SSM_WIDTH), const),
        ],
        out_specs=[pl.BlockSpec((CHUNK, SSM_WIDTH), lambda b, c: (rows(b, c), 0)),
                   pl.BlockSpec((1, SSM_WIDTH, D_STATE), lambda b, c: (b, 0, 0))],
        out_shape=[jax.ShapeDtypeStruct((batch * seq, SSM_WIDTH), BF16),
                   jax.ShapeDtypeStruct((batch, SSM_WIDTH, D_STATE), F32)],
        scratch_shapes=[pltpu.VMEM((CHUNK + 2 * SUBLANES, CONV_DIM), F32),
                        pltpu.VMEM((D_STATE, SSM_WIDTH), F32)],
        compiler_params=_params("parallel", "arbitrary"),
        name="ssd_prompt",
    )(proj, proj, proj, proj, conv_w, conv_b, dtb, alog, dskip_e, gate_norm, tril, expand)


def _rope(x, cos, sin_a, sin_b):
    parts = []
    for i in range(x.shape[1] // LANES):
        xb = x[:, i * LANES:(i + 1) * LANES]
        up = pltpu.roll(xb, LANES - ROT_DIM // 2, axis=1)
        dn = pltpu.roll(xb, ROT_DIM // 2, axis=1)
        parts.append(xb * cos + up * sin_a + dn * sin_b)
    return parts[0] if len(parts) == 1 else jnp.concatenate(parts, axis=1)


def _block_diag_rows(a, a_swapped, first):
    lane = lax.broadcasted_iota(jnp.int32, a.shape, 1)
    left = lane < HEAD_DIM
    zero = jnp.zeros_like(a)
    if first == 0:
        top, bot = jnp.where(left, a, zero), jnp.where(left, zero, a_swapped)
    else:
        top, bot = jnp.where(left, a_swapped, zero), jnp.where(left, zero, a)
    return jnp.concatenate([top, bot], axis=0).astype(BF16)


def _sink_softmax(s, mask, sink):
    s = jnp.where(mask, s, NEG_INF)
    m = jnp.maximum(jnp.max(s, axis=-1, keepdims=True), sink)
    e = jnp.exp(s - m)
    return e / (jnp.sum(e, axis=-1, keepdims=True) + jnp.exp(sink - m))


def _swa_prompt_kernel(sink_ref, q_ref, kv_ref, cos_ref, sa_ref, sb_ref,
                       att_ref, k_ref, kprev_ref, vprev_ref):
    blk = pl.program_id(1)
    w = WINDOW

    @pl.when(blk == 0)
    def _():
        kprev_ref[...] = jnp.zeros_like(kprev_ref)
        vprev_ref[...] = jnp.zeros_like(vprev_ref)

    cos, sa, sb = cos_ref[...], sa_ref[...], sb_ref[...]
    k_cur = _rope(kv_ref[:, 0:KV_WIDTH], cos, sa, sb)
    v_cur = kv_ref[:, KV_WIDTH:2 * KV_WIDTH]
    k_ref[0] = k_cur
    q = (_rope(q_ref[...], cos, sa, sb) * (HEAD_DIM ** -0.5)).astype(BF16)

    k_cat = jnp.concatenate([kprev_ref[...], k_cur], axis=0)
    v_cat = jnp.concatenate([vprev_ref[...], v_cur], axis=0)
    k_sw = pltpu.roll(k_cat, HEAD_DIM, axis=1)
    v_sw = pltpu.roll(v_cat, HEAD_DIM, axis=1)
    kbd = [_block_diag_rows(k_cat, k_sw, g) for g in range(KV_HEADS)]
    vbd = [_block_diag_rows(v_cat, v_sw, g) for g in range(KV_HEADS)]

    row = lax.broadcasted_iota(jnp.int32, (w, 2 * w), 0)
    col = lax.broadcasted_iota(jnp.int32, (w, 2 * w), 1)
    mask = jnp.logical_or(jnp.logical_and(jnp.logical_and(col < w, col > row), blk > 0),
                          jnp.logical_and(col >= w, (col - w) <= row))

    for j in range(ATT_HEADS // 2):
        g = (2 * j) // (ATT_HEADS // KV_HEADS)
        s = _dot_nt(q[:, j * LANES:(j + 1) * LANES], kbd[g])
        p = jnp.concatenate(
            [_sink_softmax(s[:, hh * 2 * w:(hh + 1) * 2 * w], mask, sink_ref[2 * j + hh])
             for hh in range(2)], axis=1)
        att_ref[:, j * LANES:(j + 1) * LANES] = _dot(p.astype(BF16), vbd[g]).astype(att_ref.dtype)

    kprev_ref[...] = k_cur
    vprev_ref[...] = v_cur


def swa_prompt(proj, sinks, cos, sin_a, sin_b, *, batch, seq):
    nb = seq // WINDOW
    rows = lambda b, i: b * nb + i
    return pl.pallas_call(
        _swa_prompt_kernel,
        grid=(batch, nb),
        in_specs=[
            pl.BlockSpec(memory_space=pltpu.SMEM),
            pl.BlockSpec((WINDOW, ATT_WIDTH), lambda b, i: (rows(b, i), COL_Q // ATT_WIDTH)),
            pl.BlockSpec((WINDOW, 2 * KV_WIDTH), lambda b, i: (rows(b, i), COL_KV // (2 * KV_WIDTH))),
            pl.BlockSpec((WINDOW, LANES), lambda b, i: (i, 0)),
            pl.BlockSpec((WINDOW, LANES), lambda b, i: (i, 0)),
            pl.BlockSpec((WINDOW, LANES), lambda b, i: (i, 0)),
        ],
        out_specs=[pl.BlockSpec((WINDOW, ATT_WIDTH), lambda b, i: (rows(b, i), 0)),
                   pl.BlockSpec((1, WINDOW, KV_WIDTH), lambda b, i: (b, 0, 0))],
        out_shape=[jax.ShapeDtypeStruct((batch * seq, ATT_WIDTH), BF16),
                   jax.ShapeDtypeStruct((batch, WINDOW, KV_WIDTH), F32)],
        scratch_shapes=[pltpu.VMEM((WINDOW, KV_WIDTH), F32),
                        pltpu.VMEM((WINDOW, KV_WIDTH), F32)],
        compiler_params=_params("parallel", "arbitrary"),
        name="swa_prompt",
    )(sinks, proj, proj, cos, sin_a, sin_b)


def _out_proj_kernel(y_ref, a_ref, x_ref, wo_ref, gx_ref, wq_ref, x1_ref, q_ref):
    mix = _dot(y_ref[...], wo_ref[0:SSM_WIDTH, :]) + _dot(a_ref[...], wo_ref[SSM_WIDTH:, :])
    x1 = x_ref[...] + mix
    x1_ref[...] = x1
    h = _rmsnorm(x1, gx_ref[...]).astype(BF16)
    q_ref[...] = _dot(h, wq_ref[...]).astype(q_ref.dtype)


def out_proj(y, att, x, w_out, norm_x, w_xq, *, tm):
    m = x.shape[0]
    const = lambda i: (0, 0)
    return pl.pallas_call(
        _out_proj_kernel,
        grid=(m // tm,),
        in_specs=[pl.BlockSpec((tm, SSM_WIDTH), lambda i: (i, 0)),
                  pl.BlockSpec((tm, ATT_WIDTH), lambda i: (i, 0)),
                  pl.BlockSpec((tm, D_MODEL), lambda i: (i, 0)),
                  pl.BlockSpec((D_MODEL, D_MODEL), const),
                  pl.BlockSpec((1, D_MODEL), const),
                  pl.BlockSpec((D_MODEL, X_WIDTH), const)],
        out_specs=[pl.BlockSpec((tm, D_MODEL), lambda i: (i, 0)),
                   pl.BlockSpec((tm, X_WIDTH), lambda i: (i, 0))],
        out_shape=[jax.ShapeDtypeStruct((m, D_MODEL), F32),
                   jax.ShapeDtypeStruct((m, X_WIDTH), BF16)],
        compiler_params=_params("parallel"),
        name="out_proj",
    )(y, att, x, w_out, norm_x, w_xq)


def _softmax(s):
    m = jnp.max(s, axis=-1, keepdims=True)
    e = jnp.exp(s - m)
    return e / jnp.sum(e, axis=-1, keepdims=True)


def _xattn_prompt_kernel(q_ref, mk_ref, mv_ref, x1_ref, wo_ref, x2_ref):
    mk = mk_ref[...].astype(BF16)
    mv = mv_ref[...].astype(BF16)
    outs = []
    for h in range(X_HEADS):
        sl = slice(h * X_HEAD_DIM, (h + 1) * X_HEAD_DIM)
        s = _dot_nt(q_ref[:, sl], mk[:, sl]) * (X_HEAD_DIM ** -0.5)
        outs.append(_dot(_softmax(s).astype(BF16), mv[:, sl]).astype(BF16))
    o = jnp.concatenate(outs, axis=1)
    x2_ref[...] = x1_ref[...] + _dot(o, wo_ref[...])


def xattn_prompt(q, mkv, x1, w_xo, *, batch, seq, tq):
    nq = seq // tq
    return pl.pallas_call(
        _xattn_prompt_kernel,
        grid=(batch, nq),
        in_specs=[pl.BlockSpec((tq, X_WIDTH), lambda b, i: (b * nq + i, 0)),
                  pl.BlockSpec((N_MEM, X_WIDTH), lambda b, i: (b, 0)),
                  pl.BlockSpec((N_MEM, X_WIDTH), lambda b, i: (b, 1)),
                  pl.BlockSpec((tq, D_MODEL), lambda b, i: (b * nq + i, 0)),
                  pl.BlockSpec((X_WIDTH, D_MODEL), lambda b, i: (0, 0))],
        out_specs=pl.BlockSpec((tq, D_MODEL), lambda b, i: (b * nq + i, 0)),
        out_shape=jax.ShapeDtypeStruct(x1.shape, F32),
        compiler_params=_params("parallel", "parallel"),
        name="xattn_prompt",
    )(q, mkv, mkv, x1, w_xo)


def _ffn_kernel(x_ref, gn_ref, wg_ref, wu_ref, wd_ref, gf_ref, o_ref, h_ref):
    f = pl.program_id(1)

    @pl.when(f == 0)
    def _():
        x = x_ref[...]
        h_ref[...] = _rmsnorm(x, gn_ref[...]).astype(BF16)
        o_ref[...] = x

    h = h_ref[...]
    act = (_silu(_dot(h, wg_ref[...])) * _dot(h, wu_ref[...])).astype(BF16)
    o_ref[...] += _dot(act, wd_ref[...])

    @pl.when(f == pl.num_programs(1) - 1)
    def _():
        o_ref[...] = _rmsnorm(o_ref[...], gf_ref[...])


def ffn(x, norm_ffn, w_gate, w_up, w_down, norm_final, *, tm, tf):
    m = x.shape[0]
    d_ff = w_gate.shape[1]
    return pl.pallas_call(
        _ffn_kernel,
        grid=(m // tm, d_ff // tf),
        in_specs=[pl.BlockSpec((tm, D_MODEL), lambda i, f: (i, 0)),
                  pl.BlockSpec((1, D_MODEL), lambda i, f: (0, 0)),
                  pl.BlockSpec((D_MODEL, tf), lambda i, f: (0, f)),
                  pl.BlockSpec((D_MODEL, tf), lambda i, f: (0, f)),
                  pl.BlockSpec((tf, D_MODEL), lambda i, f: (f, 0)),
                  pl.BlockSpec((1, D_MODEL), lambda i, f: (0, 0))],
        out_specs=pl.BlockSpec((tm, D_MODEL), lambda i, f: (i, 0)),
        out_shape=jax.ShapeDtypeStruct((m, D_MODEL), F32),
        scratch_shapes=[pltpu.VMEM((tm, D_MODEL), BF16)],
        compiler_params=_params("parallel", "arbitrary"),
        name="ffn",
    )(x, norm_ffn, w_gate, w_up, w_down, norm_final)


SEQ_TILE = SUBLANES
DEC_T = 4


def _ssd_sample_kernel(z_ref, x_ref, bc_ref, dt_ref, conv0_ref, h0_ref, cw_ref, cb_ref,
                       dtbe_ref, aloge_ref, dskip_ref, gn_ref, exp_ref,
                       y_ref, hnew_ref, conv_ref):
    nt, sb = DEC_T, SEQ_TILE
    half = SSM_WIDTH // SSM_GROUPS

    xin = [conv0_ref[j] for j in range(CONV_WIDTH - 1)]
    xin += [jnp.concatenate([x_ref[u], bc_ref[u]], axis=1) for u in range(nt)]
    for j in range(CONV_WIDTH - 1):
        conv_ref[j] = xin[nt + j]
    xbc = []
    for u in range(nt):
        acc = jnp.broadcast_to(cb_ref[...], (sb, CONV_DIM))
        for k in range(CONV_WIDTH):
            acc = acc + xin[u + k] * cw_ref[k:k + 1, :]
        xbc.append(_silu(acc))
    xs = [v[:, 0:SSM_WIDTH] for v in xbc]
    c_off = SSM_WIDTH + SSM_GROUPS * D_STATE
    b_rows = [v[:, SSM_WIDTH:c_off] for v in xbc]
    c_rows = [v[:, c_off:CONV_DIM] for v in xbc]

    dt_raw = jnp.concatenate([dt_ref[u] for u in range(nt)], axis=0)
    dt_e = _softplus(_select_matmul(dt_raw, exp_ref[...], 3) + dtbe_ref[...])
    da_e = dt_e * -jnp.exp(aloge_ref[...])
    dts = [dt_e[u * sb:(u + 1) * sb] for u in range(nt)]
    cs = []
    for u in range(nt):
        d = da_e[u * sb:(u + 1) * sb]
        cs.append(d if u == 0 else cs[-1] + d)

    lane = lax.broadcasted_iota(jnp.int32, (sb, SSM_WIDTH), 1)
    first_group = lane < half

    def group_bcast(v0, v1):
        return jnp.where(first_group, v0, v1)

    y = []
    for u in range(nt):
        acc = None
        for s in range(u + 1):
            prod = c_rows[u] * b_rows[s]
            cbv = [jnp.sum(prod[:, g * D_STATE:(g + 1) * D_STATE], axis=-1, keepdims=True)
                   for g in range(SSM_GROUPS)]
            coef = group_bcast(cbv[0], cbv[1]) * jnp.exp(cs[u] - cs[s]) * dts[s]
            term = coef * xs[s]
            acc = term if acc is None else acc + term
        y.append(acc)

    c_stack = jnp.concatenate(c_rows, axis=0).astype(BF16)
    to_end = [jnp.exp(cs[nt - 1] - cs[u]) * dts[u] for u in range(nt)]
    xw = [xs[u] * to_end[u] for u in range(nt)]
    dec_parts = [p.astype(F32) for p in _split_bf16(jnp.exp(cs[nt - 1]), 3)]
    pad_rows = jnp.zeros((sb, SSM_WIDTH), F32)
    lhs_t = jnp.concatenate(xw + dec_parts + [pad_rows], axis=0).T.astype(BF16)
    ones = jnp.ones((sb, D_STATE), F32)
    zeros = jnp.zeros((sb, D_STATE), F32)
    rhs = []
    for g in range(SSM_GROUPS):
        bg = [v[:, g * D_STATE:(g + 1) * D_STATE] for v in b_rows]
        left = jnp.concatenate(bg + [zeros] * 4, axis=0)
        right = jnp.concatenate([zeros] * nt + [ones] * 3 + [zeros], axis=0)
        rhs.append(jnp.concatenate([left, right], axis=1))
    krow = jnp.bitwise_and(lax.broadcasted_iota(jnp.int32, (2 * nt * sb, 2 * D_STATE), 0), sb - 1)
    yrow = jnp.bitwise_and(lax.broadcasted_iota(jnp.int32, (nt * sb, SSM_WIDTH), 0), sb - 1)

    y_off = jnp.zeros((nt * sb, SSM_WIDTH), F32)
    for i in range(sb):
        h0 = h0_ref[i]
        h0_b = h0.astype(BF16)
        off = jnp.concatenate(
            [_dot_nt(c_stack[:, g * D_STATE:(g + 1) * D_STATE], h0_b[g * half:(g + 1) * half])
             for g in range(SSM_GROUPS)], axis=1)
        y_off = y_off + jnp.where(yrow == i, off, 0.0)
        new = []
        for g in range(SSM_GROUPS):
            r = jnp.where(krow == i, rhs[g], 0.0).astype(BF16)
            res = _dot(lhs_t[g * half:(g + 1) * half], r)
            new.append(h0[g * half:(g + 1) * half] * res[:, D_STATE:] + res[:, :D_STATE])
        hnew_ref[i] = jnp.concatenate(new, axis=0)

    for u in range(nt):
        yu = y[u] + y_off[u * sb:(u + 1) * sb] * jnp.exp(cs[u]) + xs[u] * dskip_ref[...]
        gated = yu * _silu(z_ref[u])
        outs = [_rmsnorm(gated[:, g * half:(g + 1) * half], gn_ref[:, g * half:(g + 1) * half])
                for g in range(SSM_GROUPS)]
        y_ref[u] = jnp.concatenate(outs, axis=1).astype(y_ref.dtype)


def ssd_sample(proj, conv0, h0, conv_w, conv_b, dtb_e, alog_e, dskip_e, gate_norm, expand, *,
               batch):
    sb = SEQ_TILE
    const = lambda i: (0, 0)
    return pl.pallas_call(
        _ssd_sample_kernel,
        grid=(batch // sb,),
        in_specs=[
            pl.BlockSpec((DEC_T, sb, SSM_WIDTH), lambda i: (0, i, COL_Z // SSM_WIDTH)),
            pl.BlockSpec((DEC_T, sb, SSM_WIDTH), lambda i: (0, i, COL_X // SSM_WIDTH)),
            pl.BlockSpec((DEC_T, sb, BC_WIDTH), lambda i: (0, i, COL_BC // BC_WIDTH)),
            pl.BlockSpec((DEC_T, sb, LANES), lambda i: (0, i, COL_DT // LANES)),
            pl.BlockSpec((CONV_WIDTH - 1, sb, CONV_DIM), lambda i: (0, i, 0)),
            pl.BlockSpec((sb, SSM_WIDTH, D_STATE), lambda i: (i, 0, 0)),
            pl.BlockSpec((CONV_WIDTH, CONV_DIM), const),
            pl.BlockSpec((1, CONV_DIM), const),
            pl.BlockSpec((1, SSM_WIDTH), const),
            pl.BlockSpec((1, SSM_WIDTH), const),
            pl.BlockSpec((1, SSM_WIDTH), const),
            pl.BlockSpec((1, SSM_WIDTH), const),
            pl.BlockSpec((LANES, SSM_WIDTH), const),
        ],
        out_specs=[pl.BlockSpec((DEC_T, sb, SSM_WIDTH), lambda i: (0, i, 0)),
                   pl.BlockSpec((sb, SSM_WIDTH, D_STATE), lambda i: (i, 0, 0)),
                   pl.BlockSpec((CONV_WIDTH - 1, sb, CONV_DIM), lambda i: (0, i, 0))],
        out_shape=[jax.ShapeDtypeStruct((DEC_T, batch, SSM_WIDTH), BF16),
                   jax.ShapeDtypeStruct((batch, SSM_WIDTH, D_STATE), F32),
                   jax.ShapeDtypeStruct((CONV_WIDTH - 1, batch, CONV_DIM), F32)],
        compiler_params=_params("parallel"),
        name="ssd_sample",
    )(proj, proj, proj, proj, conv0, h0, conv_w, conv_b, dtb_e, alog_e, dskip_e, gate_norm, expand)


def _swa_sample_kernel(q_ref, kn_ref, vn_ref, kc_ref, vc_ref, sink_ref, qcos_ref, qsa_ref,
                       qsb_ref, kcos_ref, ksa_ref, ksb_ref, o_ref, ko_ref, vo_ref):
    w = WINDOW
    nt = DEC_T
    nq = KV_HEADS * nt * (ATT_HEADS // KV_HEADS)
    def token_of(rows):
        return jnp.bitwise_and(jnp.right_shift(rows, 3), nt - 1)

    row_t = token_of(lax.broadcasted_iota(jnp.int32, (nq, w), 0))
    col = lax.broadcasted_iota(jnp.int32, (nq, w), 1)
    mask_c = col > row_t
    col_n = lax.broadcasted_iota(jnp.int32, (nq, SUBLANES), 1)
    row_n = token_of(lax.broadcasted_iota(jnp.int32, (nq, SUBLANES), 0))
    mask_n = col_n <= row_n
    sink = sink_ref[...][:, 0:1]

    def body(i, carry):
        q = (_rope(q_ref[i], qcos_ref[...], qsa_ref[...], qsb_ref[...])
             * (HEAD_DIM ** -0.5)).astype(BF16)
        kn8 = _rope(kn_ref[i], kcos_ref[...], ksa_ref[...], ksb_ref[...])
        vn8 = vn_ref[i]
        kn = kn8[0:nt]
        vn = vn8[0:nt]
        kc = kc_ref[i]
        vc = vc_ref[i]
        s_c = jnp.where(mask_c, _dot_nt(q, kc.astype(BF16)), NEG_INF)
        s_n = jnp.where(mask_n, _dot_nt(q, kn8.astype(BF16)), NEG_INF)
        m = jnp.maximum(jnp.maximum(jnp.max(s_c, axis=-1, keepdims=True),
                                    jnp.max(s_n, axis=-1, keepdims=True)), sink)
        e_c = jnp.exp(s_c - m)
        e_n = jnp.exp(s_n - m)
        den = (jnp.sum(e_c, axis=-1, keepdims=True) + jnp.sum(e_n, axis=-1, keepdims=True)
               + jnp.exp(sink - m))
        o = _dot((e_c / den).astype(BF16), vc.astype(BF16)) + _dot((e_n / den).astype(BF16),
                                                                    vn8.astype(BF16))
        o_ref[i] = o.astype(o_ref.dtype)
        ko_ref[i, 0:w - nt, :] = kc_ref[i, nt:w, :]
        ko_ref[i, w - nt:w, :] = kn
        vo_ref[i, 0:w - nt, :] = vc_ref[i, nt:w, :]
        vo_ref[i, w - nt:w, :] = vn
        return carry

    lax.fori_loop(0, q_ref.shape[0], body, 0)


def swa_sample(q, k_new, v_new, k_cache, v_cache, sink_rows, qtabs, ktabs, *, batch):
    sb = SEQ_TILE
    nq = q.shape[1]
    const = lambda i: (0, 0)
    seq3 = lambda i: (i, 0, 0)
    return pl.pallas_call(
        _swa_sample_kernel,
        grid=(batch // sb,),
        in_specs=[pl.BlockSpec((sb, nq, KV_WIDTH), seq3),
                  pl.BlockSpec((sb, SUBLANES, KV_WIDTH), seq3),
                  pl.BlockSpec((sb, SUBLANES, KV_WIDTH), seq3),
                  pl.BlockSpec((sb, WINDOW, KV_WIDTH), seq3),
                  pl.BlockSpec((sb, WINDOW, KV_WIDTH), seq3),
                  pl.BlockSpec((nq, LANES), const),
                  pl.BlockSpec((nq, LANES), const),
                  pl.BlockSpec((nq, LANES), const),
                  pl.BlockSpec((nq, LANES), const),
                  pl.BlockSpec((SUBLANES, LANES), const),
                  pl.BlockSpec((SUBLANES, LANES), const),
                  pl.BlockSpec((SUBLANES, LANES), const)],
        out_specs=[pl.BlockSpec((sb, nq, KV_WIDTH), seq3),
                   pl.BlockSpec((sb, WINDOW, KV_WIDTH), seq3),
                   pl.BlockSpec((sb, WINDOW, KV_WIDTH), seq3)],
        out_shape=[jax.ShapeDtypeStruct((batch, nq, KV_WIDTH), F32),
                   jax.ShapeDtypeStruct((batch, WINDOW, KV_WIDTH), F32),
                   jax.ShapeDtypeStruct((batch, WINDOW, KV_WIDTH), F32)],
        compiler_params=_params("parallel"),
        name="swa_sample",
    )(q, k_new, v_new, k_cache, v_cache, sink_rows, *qtabs, *ktabs)


def _xattn_sample_kernel(q_ref, mk_ref, mv_ref, o_ref):
    lane_blk = jnp.right_shift(lax.broadcasted_iota(jnp.int32, (SUBLANES, X_WIDTH), 1), 7)

    def body(i, carry):
        q8 = q_ref[i]
        zero = jnp.zeros_like(q8)
        qbd = jnp.concatenate([jnp.where(lane_blk == h, q8, zero) for h in range(X_HEADS)],
                              axis=0).astype(BF16)
        s = _dot_nt(qbd, mk_ref[i].astype(BF16)) * (X_HEAD_DIM ** -0.5)
        full = _dot(_softmax(s).astype(BF16), mv_ref[i].astype(BF16))
        o_ref[i] = jnp.concatenate(
            [full[h * SUBLANES:(h + 1) * SUBLANES, h * X_HEAD_DIM:(h + 1) * X_HEAD_DIM]
             for h in range(X_HEADS)], axis=1)
        return carry

    lax.fori_loop(0, q_ref.shape[0], body, 0)


def xattn_sample(q, mk, mv, *, batch):
    sb = SEQ_TILE
    seq3 = lambda i: (i, 0, 0)
    return pl.pallas_call(
        _xattn_sample_kernel,
        grid=(batch // sb,),
        in_specs=[pl.BlockSpec((sb, SUBLANES, X_WIDTH), seq3),
                  pl.BlockSpec((sb, N_MEM, X_WIDTH), seq3),
                  pl.BlockSpec((sb, N_MEM, X_WIDTH), seq3)],
        out_specs=pl.BlockSpec((sb, SUBLANES, X_WIDTH), seq3),
        out_shape=jax.ShapeDtypeStruct((batch, SUBLANES, X_WIDTH), F32),
        compiler_params=_params("parallel"),
        name="xattn_sample",
    )(q, mk, mv)


def _matmul_residual_kernel(a_ref, w_ref, x_ref, o_ref):
    o_ref[...] = x_ref[...] + _dot(a_ref[...], w_ref[...])


def matmul_residual(a, w, x, *, tm):
    m, k = a.shape
    n = w.shape[1]
    return pl.pallas_call(
        _matmul_residual_kernel,
        grid=(m // tm,),
        in_specs=[pl.BlockSpec((tm, k), lambda i: (i, 0)),
                  pl.BlockSpec((k, n), lambda i: (0, 0)),
                  pl.BlockSpec((tm, n), lambda i: (i, 0))],
        out_specs=pl.BlockSpec((tm, n), lambda i: (i, 0)),
        out_shape=jax.ShapeDtypeStruct((m, n), F32),
        compiler_params=_params("parallel"),
        name="matmul_residual",
    )(a, w, x)


def _rope_tables(pos, reps):
    half = ROT_DIM // 2
    inv = ROPE_THETA ** (-jnp.arange(half, dtype=F32) * (2.0 / ROT_DIM))
    ang = pos.astype(F32)[:, None] * inv[None, :]
    cos, sin = jnp.cos(ang), jnp.sin(ang)
    n = pos.shape[0]
    ones = jnp.ones((n, HEAD_DIM - ROT_DIM), F32)
    zeros = jnp.zeros((n, HEAD_DIM - ROT_DIM), F32)
    z8 = jnp.zeros((n, half), F32)
    c = jnp.concatenate([cos, cos, ones], axis=1)
    sa = jnp.concatenate([-sin, z8, zeros], axis=1)
    sb = jnp.concatenate([z8, sin, zeros], axis=1)
    tile = lambda t: jnp.tile(t, (1, LANES // HEAD_DIM))
    return tuple(jnp.repeat(tile(t), reps, axis=0) if reps > 1 else tile(t) for t in (c, sa, sb))


def _permute_w_in(w_in):
    s1 = SSM_WIDTH
    s2 = s1 + CONV_DIM
    s3 = s2 + SSM_HEADS
    s4 = s3 + ATT_WIDTH
    cols = [w_in[:, 0:s1],
            w_in[:, s1:s1 + SSM_WIDTH],
            w_in[:, s3:s4],
            w_in[:, s1 + SSM_WIDTH:s2],
            w_in[:, s4:s4 + 2 * KV_WIDTH],
            w_in[:, s2:s3]]
    w = jnp.concatenate(cols, axis=1)
    return jnp.pad(w, ((0, 0), (0, PROJ_WIDTH - w.shape[1]))).astype(BF16)


def _pad_lanes(v, width=LANES):
    return jnp.pad(v, (0, width - v.shape[0]))[None, :]


def kernel(x_prompt, x_sample, mem_prompt, state_ssm, state_conv, cache_swa_k, cache_swa_v,
           cache_mem_k, cache_mem_v, norm_mix, w_in, conv_w, conv_b, dt_bias, a_log, d_skip,
           gate_norm, sinks, w_out, norm_mem, norm_x, w_xq, w_xk, w_xv, w_xo, norm_ffn,
           w_gate, w_up, w_down, norm_final):
    bp, tp, _ = x_prompt.shape
    bs, ts, _ = x_sample.shape
    assert ts == DEC_T and tp % CHUNK == 0 and bs % SEQ_TILE == 0

    w_in_p = _permute_w_in(w_in[0])
    w_out_b = w_out[0].astype(BF16)
    w_xq_b = w_xq[0].astype(BF16)
    w_xkv_b = jnp.concatenate([w_xk[0], w_xv[0]], axis=1).astype(BF16)
    w_xo_b = w_xo[0].astype(BF16)
    w_gate_b, w_up_b, w_down_b = (w[0].astype(BF16) for w in (w_gate, w_up, w_down))
    row = lambda v: v.reshape(1, -1)
    cw = jnp.concatenate([conv_w[0][:, :SSM_WIDTH], conv_w[0][:, SSM_WIDTH:]], axis=1)
    cbias = row(conv_b[0])
    dtb = _pad_lanes(dt_bias[0])
    alog = _pad_lanes(a_log[0])
    dskip_e = row(jnp.repeat(d_skip[0], SSM_HEADDIM))
    dtb_e = row(jnp.repeat(dt_bias[0], SSM_HEADDIM))
    alog_e = row(jnp.repeat(a_log[0], SSM_HEADDIM))
    gn = row(gate_norm[0])
    tril = jnp.tril(jnp.ones((CHUNK, CHUNK), F32)).astype(BF16)
    expand = (jnp.arange(LANES)[:, None] == (jnp.arange(SSM_WIDTH)[None, :] // SSM_HEADDIM)
              ).astype(BF16)

    xp = x_prompt.reshape(bp * tp, D_MODEL)
    proj_p = norm_matmul(xp, row(norm_mix[0]), w_in_p, tm=1024, tn=512)
    y_p, p_ssm = ssd_prompt(proj_p, cw, cbias, dtb, alog, dskip_e, gn, tril, expand,
                            batch=bp, seq=tp)
    cos_p, sa_p, sb_p = _rope_tables(jnp.arange(tp), 1)
    att_p, p_k = swa_prompt(proj_p, sinks[0], cos_p, sa_p, sb_p, batch=bp, seq=tp)
    x1_p, qx_p = out_proj(y_p, att_p, xp, w_out_b, row(norm_x[0]), w_xq_b, tm=512)
    mkv = norm_matmul(mem_prompt.reshape(bp * N_MEM, D_MODEL), row(norm_mem[0]), w_xkv_b,
                      tm=bp * N_MEM, tn=512)
    x2_p = xattn_prompt(qx_p, mkv, x1_p, w_xo_b, batch=bp, seq=tp, tq=512)
    y_prompt = ffn(x2_p, row(norm_ffn[0]), w_gate_b, w_up_b, w_down_b, row(norm_final),
                   tm=512, tf=512).reshape(bp, tp, D_MODEL)

    proj_p3 = proj_p.reshape(bp, tp, PROJ_WIDTH)
    tail = proj_p3[:, tp - (CONV_WIDTH - 1):, :]
    p_conv = jnp.concatenate([tail[:, :, COL_X:COL_X + SSM_WIDTH],
                              tail[:, :, COL_BC:COL_BC + BC_WIDTH]], axis=-1)
    p_v = proj_p3[:, tp - WINDOW:, COL_KV + KV_WIDTH:COL_KV + 2 * KV_WIDTH]
    p_mk = mkv[:, :X_WIDTH].reshape(bp, N_MEM, X_HEADS, X_HEAD_DIM)
    p_mv = mkv[:, X_WIDTH:].reshape(bp, N_MEM, X_HEADS, X_HEAD_DIM)

    xs = x_sample.transpose(1, 0, 2).reshape(ts * bs, D_MODEL)
    proj_s = norm_matmul(xs, row(norm_mix[0]), w_in_p, tm=ts * bs, tn=512)
    proj_s3 = proj_s.reshape(ts, bs, PROJ_WIDTH)
    y_s, s_ssm, s_conv = ssd_sample(
        proj_s3, state_conv[0].transpose(1, 0, 2),
        state_ssm[0].reshape(bs, SSM_WIDTH, D_STATE), cw, cbias, dtb_e, alog_e, dskip_e, gn,
        expand, batch=bs)

    rep = ATT_HEADS // KV_HEADS
    q_s = proj_s3[:, :, COL_Q:COL_Q + ATT_WIDTH].reshape(ts, bs, KV_HEADS, rep, HEAD_DIM)
    q_s = q_s.transpose(1, 2, 0, 3, 4)
    zq = jnp.zeros_like(q_s[:, 0])
    q_bd = jnp.stack([jnp.concatenate([q_s[:, 0], zq], axis=-1),
                      jnp.concatenate([zq, q_s[:, 1]], axis=-1)], axis=1)
    q_bd = q_bd.reshape(bs, KV_HEADS * ts * rep, KV_WIDTH)
    kv_s = proj_s3[:, :, COL_KV:COL_KV + 2 * KV_WIDTH].transpose(1, 0, 2)
    kv_s = jnp.pad(kv_s, ((0, 0), (0, SUBLANES - ts), (0, 0)))
    pos_s = PAST_LEN + jnp.arange(ts)
    ktabs = _rope_tables(PAST_LEN + jnp.arange(SUBLANES), 1)
    qtabs = tuple(jnp.tile(t, (KV_HEADS, 1)) for t in _rope_tables(pos_s, rep))
    sink_rows = jnp.broadcast_to(
        jnp.broadcast_to(sinks[0].reshape(KV_HEADS, 1, rep), (KV_HEADS, ts, rep)).reshape(-1, 1),
        (KV_HEADS * ts * rep, LANES))
    o_bd, s_k, s_v = swa_sample(
        q_bd, kv_s[:, :, :KV_WIDTH], kv_s[:, :, KV_WIDTH:],
        cache_swa_k[0].reshape(bs, WINDOW, KV_WIDTH), cache_swa_v[0].reshape(bs, WINDOW, KV_WIDTH),
        sink_rows, qtabs, ktabs, batch=bs)
    o_bd = o_bd.reshape(bs, KV_HEADS, ts, rep, KV_HEADS, HEAD_DIM)
    att_s = jnp.stack([o_bd[:, g, :, :, g, :] for g in range(KV_HEADS)], axis=2)
    att_s = att_s.transpose(1, 0, 2, 3, 4).reshape(ts * bs, ATT_WIDTH).astype(BF16)

    x1_s, qx_s = out_proj(y_s.reshape(ts * bs, SSM_WIDTH), att_s, xs, w_out_b, row(norm_x[0]),
                          w_xq_b, tm=ts * bs)
    qx_s8 = jnp.pad(qx_s.astype(F32).reshape(ts, bs, X_WIDTH).transpose(1, 0, 2),
                    ((0, 0), (0, SUBLANES - ts), (0, 0)))
    o_x = xattn_sample(qx_s8, cache_mem_k[0].reshape(bs, N_MEM, X_WIDTH),
                       cache_mem_v[0].reshape(bs, N_MEM, X_WIDTH), batch=bs)
    o_x = o_x[:, :ts].transpose(1, 0, 2).reshape(ts * bs, X_WIDTH).astype(BF16)
    x2_s = matmul_residual(o_x, w_xo_b, x1_s, tm=ts * bs)
    y_s_out = ffn(x2_s, row(norm_ffn[0]), w_gate_b, w_up_b, w_down_b, row(norm_final),
                  tm=ts * bs, tf=512)
    y_sample = y_s_out.reshape(ts, bs, D_MODEL).transpose(1, 0, 2)

    return (y_prompt, y_sample,
            p_ssm.reshape(1, bp, SSM_HEADS, SSM_HEADDIM, D_STATE), p_conv[None],
            p_k.reshape(1, bp, WINDOW, KV_HEADS, HEAD_DIM),
            p_v.reshape(1, bp, WINDOW, KV_HEADS, HEAD_DIM),
            p_mk[None], p_mv[None],
            s_ssm.reshape(1, bs, SSM_HEADS, SSM_HEADDIM, D_STATE),
            s_conv.transpose(1, 0, 2)[None],
            s_k.reshape(1, bs, WINDOW, KV_HEADS, HEAD_DIM),
            s_v.reshape(1, bs, WINDOW, KV_HEADS, HEAD_DIM))
```

```python
import functools

import jax
import jax.numpy as jnp
import numpy as np
from jax import lax
from jax.experimental import pallas as pl
from jax.experimental.pallas import tpu as pltpu

F32 = jnp.float32
BF16 = jnp.bfloat16

D_MODEL = 2048
SSM_WIDTH = 1024
SSM_HEADDIM = 64
SSM_HEADS = 16
SSM_GROUPS = 2
D_STATE = 128
CONV_WIDTH = 4
CONV_DIM = SSM_WIDTH + 2 * SSM_GROUPS * D_STATE
BC_WIDTH = 2 * SSM_GROUPS * D_STATE
ATT_WIDTH = 1024
HEAD_DIM = 64
ATT_HEADS = 16
KV_HEADS = 2
KV_WIDTH = KV_HEADS * HEAD_DIM
WINDOW = 128
CHUNK = 128
ROT_DIM = 16
ROPE_THETA = 500000.0
PAST_LEN = 16384
N_MEM = 256
X_HEADS = 4
X_HEAD_DIM = 128
X_WIDTH = X_HEADS * X_HEAD_DIM
EPS = 1e-5

LANES = 128
SUBLANES = 8
VMEM_LIMIT_BYTES = 56 * 1024 * 1024

PROJ_WIDTH = 4096
COL_Z = 0
COL_X = 1024
COL_Q = 2048
COL_BC = 3072
COL_KV = 3584
COL_DT = 3840

NEG_INF = float("-inf")


def _params(*semantics):
    return pltpu.CompilerParams(dimension_semantics=semantics,
                                vmem_limit_bytes=VMEM_LIMIT_BYTES)


def _rmsnorm(x, gain):
    ms = jnp.mean(x * x, axis=-1, keepdims=True)
    return x * lax.rsqrt(ms + EPS) * gain


def _silu(x):
    return x * jax.nn.sigmoid(x)


def _softplus(x):
    return jnp.maximum(x, 0.0) + jnp.log1p(jnp.exp(-jnp.abs(x)))


def _split_bf16(v, parts):
    out = []
    rem = v
    for _ in range(parts - 1):
        p = rem.astype(BF16)
        out.append(p)
        rem = rem - p.astype(F32)
    out.append(rem.astype(BF16))
    return out


def _dot(a, b):
    return jnp.dot(a, b, preferred_element_type=F32)


def _dot_nt(a, b):
    return lax.dot_general(a, b, (((1,), (1,)), ((), ())), preferred_element_type=F32)


def _dot_tn(a, b):
    return lax.dot_general(a, b, (((0,), (0,)), ((), ())), preferred_element_type=F32)


def _select_matmul(v, sel, parts):
    acc = None
    for p in _split_bf16(v, parts):
        t = _dot(p, sel)
        acc = t if acc is None else acc + t
    return acc


def _norm_matmul_kernel(x_ref, g_ref, w_ref, o_ref, h_ref):
    @pl.when(pl.program_id(1) == 0)
    def _():
        h_ref[...] = _rmsnorm(x_ref[...], g_ref[...]).astype(BF16)

    o_ref[...] = _dot(h_ref[...], w_ref[...]).astype(o_ref.dtype)


def norm_matmul(x, gain, w, *, tm, tn, out_dtype=F32):
    m, k = x.shape
    n = w.shape[1]
    return pl.pallas_call(
        _norm_matmul_kernel,
        grid=(m // tm, n // tn),
        in_specs=[pl.BlockSpec((tm, k), lambda i, j: (i, 0)),
                  pl.BlockSpec((1, k), lambda i, j: (0, 0)),
                  pl.BlockSpec((k, tn), lambda i, j: (0, j))],
        out_specs=pl.BlockSpec((tm, tn), lambda i, j: (i, j)),
        out_shape=jax.ShapeDtypeStruct((m, n), out_dtype),
        scratch_shapes=[pltpu.VMEM((tm, k), BF16)],
        compiler_params=_params("parallel", "arbitrary"),
        name="norm_matmul",
    )(x, gain, w)


def _ssd_prompt_kernel(z_ref, x_ref, bc_ref, dt_ref, cw_ref, cb_ref, dtb_ref, alog_ref,
                       dskip_ref, gn_ref, tril_ref, exp_ref,
                       y_ref, state_ref, xpad_ref, st_ref):
    c = pl.program_id(1)
    nc = pl.num_programs(1)
    t = CHUNK

    @pl.when(c == 0)
    def _():
        xpad_ref[0:SUBLANES, :] = jnp.zeros((SUBLANES, CONV_DIM), F32)
        st_ref[...] = jnp.zeros_like(st_ref)

    xpad_ref[SUBLANES:SUBLANES + t, 0:SSM_WIDTH] = x_ref[...]
    xpad_ref[SUBLANES:SUBLANES + t, SSM_WIDTH:CONV_DIM] = bc_ref[...]
    acc = jnp.broadcast_to(cb_ref[...], (t, CONV_DIM))
    for k in range(CONV_WIDTH):
        lo = SUBLANES - (CONV_WIDTH - 1) + k
        acc = acc + xpad_ref[lo:lo + t, :] * cw_ref[k:k + 1, :]
    xbc = _silu(acc)
    xpad_ref[0:SUBLANES, :] = xpad_ref[t:t + SUBLANES, :]

    xs = xbc[:, 0:SSM_WIDTH]
    xs_b = xs.astype(BF16)
    b_mat = [xbc[:, SSM_WIDTH + g * D_STATE:SSM_WIDTH + (g + 1) * D_STATE].astype(BF16)
             for g in range(SSM_GROUPS)]
    c_off = SSM_WIDTH + SSM_GROUPS * D_STATE
    c_mat = [xbc[:, c_off + g * D_STATE:c_off + (g + 1) * D_STATE].astype(BF16)
             for g in range(SSM_GROUPS)]

    dt = _softplus(dt_ref[...] + dtb_ref[...])
    a = -jnp.exp(alog_ref[...])
    da = dt * a
    cs = _select_matmul_left(tril_ref[...], da)
    cs_t = cs.T
    dt_t = dt.T
    cs_last = cs[t - 1:t, :]
    exp_cs = jnp.exp(cs)
    to_end = jnp.exp(cs_last - cs) * dt
    sel = exp_ref[...]
    exp_cs_e = _select_matmul(exp_cs, sel, 2)
    to_end_e = _select_matmul(to_end, sel, 2)

    row = lax.broadcasted_iota(jnp.int32, (t, 2 * t), 0)
    col = lax.broadcasted_iota(jnp.int32, (t, 2 * t), 1)
    causal = jnp.bitwise_and(col, t - 1) <= row
    lane = lax.broadcasted_iota(jnp.int32, (t, LANES), 1)
    left = lane < SSM_HEADDIM

    cb = [_dot_nt(c_mat[g], b_mat[g]) for g in range(SSM_GROUPS)]
    cb2 = [jnp.concatenate([m, m], axis=1) for m in cb]

    y_parts = []
    for j in range(SSM_HEADS // 2):
        g = (2 * j) // (SSM_HEADS // SSM_GROUPS)
        h0, h1 = 2 * j, 2 * j + 1
        colv = jnp.concatenate([jnp.broadcast_to(cs[:, h0:h0 + 1], (t, t)),
                                jnp.broadcast_to(cs[:, h1:h1 + 1], (t, t))], axis=1)
        rowv = jnp.concatenate([jnp.broadcast_to(cs_t[h0:h0 + 1, :], (t, t)),
                                jnp.broadcast_to(cs_t[h1:h1 + 1, :], (t, t))], axis=1)
        dtr = jnp.concatenate([jnp.broadcast_to(dt_t[h0:h0 + 1, :], (t, t)),
                               jnp.broadcast_to(dt_t[h1:h1 + 1, :], (t, t))], axis=1)
        decay = jnp.where(causal, jnp.exp(colv - rowv), 0.0)
        wts = (cb2[g] * decay * dtr).astype(BF16)
        xp = xs_b[:, j * LANES:(j + 1) * LANES]
        zero = jnp.zeros_like(xp)
        rhs = jnp.concatenate([jnp.where(left, xp, zero), jnp.where(left, zero, xp)], axis=0)
        y_parts.append(_dot(wts, rhs))
    y_diag = jnp.concatenate(y_parts, axis=1)

    half = SSM_WIDTH // SSM_GROUPS
    st = st_ref[...]
    st_b = st.astype(BF16)
    y_off = jnp.concatenate(
        [_dot(c_mat[g], st_b[:, g * half:(g + 1) * half]) for g in range(SSM_GROUPS)], axis=1)
    xw = (xs * to_end_e).astype(BF16)
    upd = jnp.concatenate(
        [_dot_tn(b_mat[g], xw[:, g * half:(g + 1) * half]) for g in range(SSM_GROUPS)], axis=1)
    st_new = st * exp_cs_e[t - 1:t, :] + upd
    st_ref[...] = st_new

    y = y_diag + y_off * exp_cs_e + xs * dskip_ref[...]
    gated = y * _silu(z_ref[...])
    outs = []
    for g in range(SSM_GROUPS):
        outs.append(_rmsnorm(gated[:, g * half:(g + 1) * half], gn_ref[:, g * half:(g + 1) * half]))
    y_ref[...] = jnp.concatenate(outs, axis=1).astype(y_ref.dtype)

    @pl.when(c == nc - 1)
    def _():
        state_ref[0] = st_new.T


def _select_matmul_left(sel, v):
    acc = None
    for p in _split_bf16(v, 3):
        t = _dot(sel, p)
        acc = t if acc is None else acc + t
    return acc


def ssd_prompt(proj, conv_w, conv_b, dtb, alog, dskip_e, gate_norm, tril, expand, *, batch, seq):
    nc = seq // CHUNK
    rows = lambda b, c: b * nc + c
    const = lambda b, c: (0, 0)
    return pl.pallas_call(
        _ssd_prompt_kernel,
        grid=(batch, nc),
        in_specs=[
            pl.BlockSpec((CHUNK, SSM_WIDTH), lambda b, c: (rows(b, c), COL_Z // SSM_WIDTH)),
            pl.BlockSpec((CHUNK, SSM_WIDTH), lambda b, c: (rows(b, c), COL_X // SSM_WIDTH)),
            pl.BlockSpec((CHUNK, BC_WIDTH), lambda b, c: (rows(b, c), COL_BC // BC_WIDTH)),
            pl.BlockSpec((CHUNK, LANES), lambda b, c: (rows(b, c), COL_DT // LANES)),
            pl.BlockSpec((CONV_WIDTH, CONV_DIM), const),
            pl.BlockSpec((1, CONV_DIM), const),
            pl.BlockSpec((1, LANES), const),
            pl.BlockSpec((1, LANES), const),
            pl.BlockSpec((1, SSM_WIDTH), const),
            pl.BlockSpec((1, SSM_WIDTH), const),
            pl.BlockSpec((CHUNK, CHUNK), const),
            pl.BlockSpec((LANES, SSM_WIDTH), const),
        ],
        out_specs=[pl.BlockSpec((CHUNK, SSM_WIDTH), lambda b, c: (rows(b, c), 0)),
                   pl.BlockSpec((1, SSM_WIDTH, D_STATE), lambda b, c: (b, 0, 0))],
        out_shape=[jax.ShapeDtypeStruct((batch * seq, SSM_WIDTH), BF16),
                   jax.ShapeDtypeStruct((batch, SSM_WIDTH, D_STATE), F32)],
        scratch_shapes=[pltpu.VMEM((CHUNK + 2 * SUBLANES, CONV_DIM), F32),
                        pltpu.VMEM((D_STATE, SSM_WIDTH), F32)],
        compiler_params=_params("parallel", "arbitrary"),
        name="ssd_prompt",
    )(proj, proj, proj, proj, conv_w, conv_b, dtb, alog, dskip_e, gate_norm, tril, expand)


def _rope(x, cos, sin_a, sin_b):
    parts = []
    for i in range(x.shape[1] // LANES):
        xb = x[:, i * LANES:(i + 1) * LANES]
        up = pltpu.roll(xb, LANES - ROT_DIM // 2, axis=1)
        dn = pltpu.roll(xb, ROT_DIM // 2, axis=1)
        parts.append(xb * cos + up * sin_a + dn * sin_b)
    return parts[0] if len(parts) == 1 else jnp.concatenate(parts, axis=1)


def _block_diag_rows(a, a_swapped, first):
    lane = lax.broadcasted_iota(jnp.int32, a.shape, 1)
    left = lane < HEAD_DIM
    zero = jnp.zeros_like(a)
    if first == 0:
        top, bot = jnp.where(left, a, zero), jnp.where(left, zero, a_swapped)
    else:
        top, bot = jnp.where(left, a_swapped, zero), jnp.where(left, zero, a)
    return jnp.concatenate([top, bot], axis=0)


def _swa_prompt_kernel(sink_ref, q_ref, kv_ref, cos_ref, sa_ref, sb_ref,
                       att_ref, k_ref, kprev_ref, vprev_ref, s_ref, p_ref):
    blk = pl.program_id(1)
    w = WINDOW
    pairs = ATT_HEADS // 2

    @pl.when(blk == 0)
    def _():
        kprev_ref[...] = jnp.zeros_like(kprev_ref)
        vprev_ref[...] = jnp.zeros_like(vprev_ref)

    cos, sa, sb = cos_ref[...], sa_ref[...], sb_ref[...]
    k_cur = _rope(kv_ref[:, 0:KV_WIDTH], cos, sa, sb)
    v_cur = kv_ref[:, KV_WIDTH:2 * KV_WIDTH]
    k_ref[0] = k_cur
    q = (_rope(q_ref[...], cos, sa, sb) * (HEAD_DIM ** -0.5)).astype(BF16)

    k_cat = jnp.concatenate([kprev_ref[...], k_cur], axis=0)
    v_cat = jnp.concatenate([vprev_ref[...], v_cur], axis=0)
    k_sw = pltpu.roll(k_cat, HEAD_DIM, axis=1)
    v_sw = pltpu.roll(v_cat, HEAD_DIM, axis=1)
    kbd = [_block_diag_rows(k_cat, k_sw, g).astype(BF16) for g in range(KV_HEADS)]
    vbd_t = [_block_diag_rows(v_cat, v_sw, g).T.astype(BF16) for g in range(KV_HEADS)]

    for j in range(pairs):
        g = (2 * j) // (ATT_HEADS // KV_HEADS)
        s_ref[j] = _dot_nt(kbd[g], q[:, j * LANES:(j + 1) * LANES])

    key = lax.broadcasted_iota(jnp.int32, (w, w), 0)
    qry = lax.broadcasted_iota(jnp.int32, (w, w), 1)
    own_valid = key <= qry
    has_prev = blk > 0
    for j in range(pairs):
        for hh in range(2):
            base = hh * 2 * w
            sink = sink_ref[2 * j + hh]
            prev = jnp.where(has_prev, s_ref[j, base:base + w, :], NEG_INF)
            s = jnp.where(own_valid, s_ref[j, base + w:base + 2 * w, :], prev)
            m = jnp.maximum(jnp.max(s, axis=0, keepdims=True), sink)
            e = jnp.exp(s - m)
            den = jnp.sum(e, axis=0, keepdims=True) + jnp.exp(sink - m)
            p = e * (1.0 / den)
            p_ref[j, base:base + w, :] = jnp.where(own_valid, 0.0, p).astype(BF16)
            p_ref[j, base + w:base + 2 * w, :] = jnp.where(own_valid, p, 0.0).astype(BF16)

    for j in range(pairs):
        g = (2 * j) // (ATT_HEADS // KV_HEADS)
        o_t = _dot(vbd_t[g], p_ref[j])
        att_ref[:, j * LANES:(j + 1) * LANES] = o_t.T.astype(att_ref.dtype)

    kprev_ref[...] = k_cur
    vprev_ref[...] = v_cur


def swa_prompt(proj, sinks, cos, sin_a, sin_b, *, batch, seq):
    nb = seq // WINDOW
    rows = lambda b, i: b * nb + i
    return pl.pallas_call(
        _swa_prompt_kernel,
        grid=(batch, nb),
        in_specs=[
            pl.BlockSpec(memory_space=pltpu.SMEM),
            pl.BlockSpec((WINDOW, ATT_WIDTH), lambda b, i: (rows(b, i), COL_Q // ATT_WIDTH)),
            pl.BlockSpec((WINDOW, 2 * KV_WIDTH), lambda b, i: (rows(b, i), COL_KV // (2 * KV_WIDTH))),
            pl.BlockSpec((WINDOW, LANES), lambda b, i: (i, 0)),
            pl.BlockSpec((WINDOW, LANES), lambda b, i: (i, 0)),
            pl.BlockSpec((WINDOW, LANES), lambda b, i: (i, 0)),
        ],
        out_specs=[pl.BlockSpec((WINDOW, ATT_WIDTH), lambda b, i: (rows(b, i), 0)),
                   pl.BlockSpec((1, WINDOW, KV_WIDTH), lambda b, i: (b, 0, 0))],
        out_shape=[jax.ShapeDtypeStruct((batch * seq, ATT_WIDTH), BF16),
                   jax.ShapeDtypeStruct((batch, WINDOW, KV_WIDTH), F32)],
        scratch_shapes=[pltpu.VMEM((WINDOW, KV_WIDTH), F32),
                        pltpu.VMEM((WINDOW, KV_WIDTH), F32),
                        pltpu.VMEM((ATT_HEADS // 2, 4 * WINDOW, WINDOW), F32),
                        pltpu.VMEM((ATT_HEADS // 2, 4 * WINDOW, WINDOW), BF16)],
        compiler_params=_params("parallel", "arbitrary"),
        name="swa_prompt",
    )(sinks, proj, proj, cos, sin_a, sin_b)


def _out_proj_kernel(y_ref, a_ref, x_ref, wo_ref, gx_ref, wq_ref, x1_ref, q_ref):
    mix = _dot(y_ref[...], wo_ref[0:SSM_WIDTH, :]) + _dot(a_ref[...], wo_ref[SSM_WIDTH:, :])
    x1 = x_ref[...] + mix
    x1_ref[...] = x1
    h = _rmsnorm(x1, gx_ref[...]).astype(BF16)
    q_ref[...] = _dot(h, wq_ref[...]).astype(q_ref.dtype)


def out_proj(y, att, x, w_out, norm_x, w_xq, *, tm):
    m = x.shape[0]
    const = lambda i: (0, 0)
    return pl.pallas_call(
        _out_proj_kernel,
        grid=(m // tm,),
        in_specs=[pl.BlockSpec((tm, SSM_WIDTH), lambda i: (i, 0)),
                  pl.BlockSpec((tm, ATT_WIDTH), lambda i: (i, 0)),
                  pl.BlockSpec((tm, D_MODEL), lambda i: (i, 0)),
                  pl.BlockSpec((D_MODEL, D_MODEL), const),
                  pl.BlockSpec((1, D_MODEL), const),
                  pl.BlockSpec((D_MODEL, X_WIDTH), const)],
        out_specs=[pl.BlockSpec((tm, D_MODEL), lambda i: (i, 0)),
                   pl.BlockSpec((tm, X_WIDTH), lambda i: (i, 0))],
        out_shape=[jax.ShapeDtypeStruct((m, D_MODEL), F32),
                   jax.ShapeDtypeStruct((m, X_WIDTH), BF16)],
        compiler_params=_params("parallel"),
        name="out_proj",
    )(y, att, x, w_out, norm_x, w_xq)


def _softmax(s):
    m = jnp.max(s, axis=-1, keepdims=True)
    e = jnp.exp(s - m)
    return e / jnp.sum(e, axis=-1, keepdims=True)


def _xattn_prompt_kernel(q_ref, mk_ref, mv_ref, x1_ref, wo_ref, x2_ref):
    mk = mk_ref[...].astype(BF16)
    mv = mv_ref[...].astype(BF16)
    outs = []
    for h in range(X_HEADS):
        sl = slice(h * X_HEAD_DIM, (h + 1) * X_HEAD_DIM)
        s = _dot_nt(q_ref[:, sl], mk[:, sl]) * (X_HEAD_DIM ** -0.5)
        outs.append(_dot(_softmax(s).astype(BF16), mv[:, sl]).astype(BF16))
    o = jnp.concatenate(outs, axis=1)
    x2_ref[...] = x1_ref[...] + _dot(o, wo_ref[...])


def xattn_prompt(q, mkv, x1, w_xo, *, batch, seq, tq):
    nq = seq // tq
    return pl.pallas_call(
        _xattn_prompt_kernel,
        grid=(batch, nq),
        in_specs=[pl.BlockSpec((tq, X_WIDTH), lambda b, i: (b * nq + i, 0)),
                  pl.BlockSpec((N_MEM, X_WIDTH), lambda b, i: (b, 0)),
                  pl.BlockSpec((N_MEM, X_WIDTH), lambda b, i: (b, 1)),
                  pl.BlockSpec((tq, D_MODEL), lambda b, i: (b * nq + i, 0)),
                  pl.BlockSpec((X_WIDTH, D_MODEL), lambda b, i: (0, 0))],
        out_specs=pl.BlockSpec((tq, D_MODEL), lambda b, i: (b * nq + i, 0)),
        out_shape=jax.ShapeDtypeStruct(x1.shape, F32),
        compiler_params=_params("parallel", "parallel"),
        name="xattn_prompt",
    )(q, mkv, mkv, x1, w_xo)


def _ffn_kernel(x_ref, gn_ref, wg_ref, wu_ref, wd_ref, gf_ref, o_ref, h_ref):
    f = pl.program_id(1)

    @pl.when(f == 0)
    def _():
        x = x_ref[...]
        h_ref[...] = _rmsnorm(x, gn_ref[...]).astype(BF16)
        o_ref[...] = x

    h = h_ref[...]
    act = (_silu(_dot(h, wg_ref[...])) * _dot(h, wu_ref[...])).astype(BF16)
    o_ref[...] += _dot(act, wd_ref[...])

    @pl.when(f == pl.num_programs(1) - 1)
    def _():
        o_ref[...] = _rmsnorm(o_ref[...], gf_ref[...])


def ffn(x, norm_ffn, w_gate, w_up, w_down, norm_final, *, tm, tf):
    m = x.shape[0]
    d_ff = w_gate.shape[1]
    return pl.pallas_call(
        _ffn_kernel,
        grid=(m // tm, d_ff // tf),
        in_specs=[pl.BlockSpec((tm, D_MODEL), lambda i, f: (i, 0)),
                  pl.BlockSpec((1, D_MODEL), lambda i, f: (0, 0)),
                  pl.BlockSpec((D_MODEL, tf), lambda i, f: (0, f)),
                  pl.BlockSpec((D_MODEL, tf), lambda i, f: (0, f)),
                  pl.BlockSpec((tf, D_MODEL), lambda i, f: (f, 0)),
                  pl.BlockSpec((1, D_MODEL), lambda i, f: (0, 0))],
        out_specs=pl.BlockSpec((tm, D_MODEL), lambda i, f: (i, 0)),
        out_shape=jax.ShapeDtypeStruct((m, D_MODEL), F32),
        scratch_shapes=[pltpu.VMEM((tm, D_MODEL), BF16)],
        compiler_params=_params("parallel", "arbitrary"),
        name="ffn",
    )(x, norm_ffn, w_gate, w_up, w_down, norm_final)


SEQ_TILE = SUBLANES
DEC_T = 4


def _ssd_sample_kernel(z_ref, x_ref, bc_ref, dt_ref, conv0_ref, h0_ref, cw_ref, cb_ref,
                       dtbe_ref, aloge_ref, dskip_ref, gn_ref, exp_ref,
                       y_ref, hnew_ref, conv_ref):
    nt, sb = DEC_T, SEQ_TILE
    half = SSM_WIDTH // SSM_GROUPS

    xin = [conv0_ref[j] for j in range(CONV_WIDTH - 1)]
    xin += [jnp.concatenate([x_ref[u], bc_ref[u]], axis=1) for u in range(nt)]
    for j in range(CONV_WIDTH - 1):
        conv_ref[j] = xin[nt + j]
    xbc = []
    for u in range(nt):
        acc = jnp.broadcast_to(cb_ref[...], (sb, CONV_DIM))
        for k in range(CONV_WIDTH):
            acc = acc + xin[u + k] * cw_ref[k:k + 1, :]
        xbc.append(_silu(acc))
    xs = [v[:, 0:SSM_WIDTH] for v in xbc]
    c_off = SSM_WIDTH + SSM_GROUPS * D_STATE
    b_rows = [v[:, SSM_WIDTH:c_off] for v in xbc]
    c_rows = [v[:, c_off:CONV_DIM] for v in xbc]

    dt_raw = jnp.concatenate([dt_ref[u] for u in range(nt)], axis=0)
    dt_e = _softplus(_select_matmul(dt_raw, exp_ref[...], 3) + dtbe_ref[...])
    da_e = dt_e * -jnp.exp(aloge_ref[...])
    dts = [dt_e[u * sb:(u + 1) * sb] for u in range(nt)]
    cs = []
    for u in range(nt):
        d = da_e[u * sb:(u + 1) * sb]
        cs.append(d if u == 0 else cs[-1] + d)

    lane = lax.broadcasted_iota(jnp.int32, (sb, SSM_WIDTH), 1)
    first_group = lane < half

    def group_bcast(v0, v1):
        return jnp.where(first_group, v0, v1)

    y = []
    for u in range(nt):
        acc = None
        for s in range(u + 1):
            prod = c_rows[u] * b_rows[s]
            cbv = [jnp.sum(prod[:, g * D_STATE:(g + 1) * D_STATE], axis=-1, keepdims=True)
                   for g in range(SSM_GROUPS)]
            coef = group_bcast(cbv[0], cbv[1]) * jnp.exp(cs[u] - cs[s]) * dts[s]
            term = coef * xs[s]
            acc = term if acc is None else acc + term
        y.append(acc)

    c_stack = jnp.concatenate(c_rows, axis=0).astype(BF16)
    to_end = [jnp.exp(cs[nt - 1] - cs[u]) * dts[u] for u in range(nt)]
    xw = [xs[u] * to_end[u] for u in range(nt)]
    dec_parts = [p.astype(F32) for p in _split_bf16(jnp.exp(cs[nt - 1]), 3)]
    pad_rows = jnp.zeros((sb, SSM_WIDTH), F32)
    lhs_t = jnp.concatenate(xw + dec_parts + [pad_rows], axis=0).T.astype(BF16)
    ones = jnp.ones((sb, D_STATE), F32)
    zeros = jnp.zeros((sb, D_STATE), F32)
    rhs = []
    for g in range(SSM_GROUPS):
        bg = [v[:, g * D_STATE:(g + 1) * D_STATE] for v in b_rows]
        left = jnp.concatenate(bg + [zeros] * 4, axis=0)
        right = jnp.concatenate([zeros] * nt + [ones] * 3 + [zeros], axis=0)
        rhs.append(jnp.concatenate([left, right], axis=1))
    krow = jnp.bitwise_and(lax.broadcasted_iota(jnp.int32, (2 * nt * sb, 2 * D_STATE), 0), sb - 1)
    yrow = jnp.bitwise_and(lax.broadcasted_iota(jnp.int32, (nt * sb, SSM_WIDTH), 0), sb - 1)

    y_off = jnp.zeros((nt * sb, SSM_WIDTH), F32)
    for i in range(sb):
        h0 = h0_ref[i]
        h0_b = h0.astype(BF16)
        off = jnp.concatenate(
            [_dot_nt(c_stack[:, g * D_STATE:(g + 1) * D_STATE], h0_b[g * half:(g + 1) * half])
             for g in range(SSM_GROUPS)], axis=1)
        y_off = y_off + jnp.where(yrow == i, off, 0.0)
        new = []
        for g in range(SSM_GROUPS):
            r = jnp.where(krow == i, rhs[g], 0.0).astype(BF16)
            res = _dot(lhs_t[g * half:(g + 1) * half], r)
            new.append(h0[g * half:(g + 1) * half] * res[:, D_STATE:] + res[:, :D_STATE])
        hnew_ref[i] = jnp.concatenate(new, axis=0)

    for u in range(nt):
        yu = y[u] + y_off[u * sb:(u + 1) * sb] * jnp.exp(cs[u]) + xs[u] * dskip_ref[...]
        gated = yu * _silu(z_ref[u])
        outs = [_rmsnorm(gated[:, g * half:(g + 1) * half], gn_ref[:, g * half:(g + 1) * half])
                for g in range(SSM_GROUPS)]
        y_ref[u] = jnp.concatenate(outs, axis=1).astype(y_ref.dtype)


def ssd_sample(proj, conv0, h0, conv_w, conv_b, dtb_e, alog_e, dskip_e, gate_norm, expand, *,
               batch):
    sb = SEQ_TILE
    const = lambda i: (0, 0)
    return pl.pallas_call(
        _ssd_sample_kernel,
        grid=(batch // sb,),
        in_specs=[
            pl.BlockSpec((DEC_T, sb, SSM_WIDTH), lambda i: (0, i, COL_Z // SSM_WIDTH)),
            pl.BlockSpec((DEC_T, sb, SSM_WIDTH), lambda i: (0, i, COL_X // SSM_WIDTH)),
            pl.BlockSpec((DEC_T, sb, BC_WIDTH), lambda i: (0, i, COL_BC // BC_WIDTH)),
            pl.BlockSpec((DEC_T, sb, LANES), lambda i: (0, i, COL_DT // LANES)),
            pl.BlockSpec((CONV_WIDTH - 1, sb, CONV_DIM), lambda i: (0, i, 0)),
            pl.BlockSpec((sb, SSM_WIDTH, D_STATE), lambda i: (i, 0, 0)),
            pl.BlockSpec((CONV_WIDTH, CONV_DIM), const),
            pl.BlockSpec((1, CONV_DIM), const),
            pl.BlockSpec((1, SSM_WIDTH), const),
            pl.BlockSpec((1, SSM_WIDTH), const),
            pl.BlockSpec((1, SSM_WIDTH), const),
            pl.BlockSpec((1, SSM_WIDTH), const),
            pl.BlockSpec((LANES, SSM_WIDTH), const),
        ],
        out_specs=[pl.BlockSpec((DEC_T, sb, SSM_WIDTH), lambda i: (0, i, 0)),
                   pl.BlockSpec((sb, SSM_WIDTH, D_STATE), lambda i: (i, 0, 0)),
                   pl.BlockSpec((CONV_WIDTH - 1, sb, CONV_DIM), lambda i: (0, i, 0))],
        out_shape=[jax.ShapeDtypeStruct((DEC_T, batch, SSM_WIDTH), BF16),
                   jax.ShapeDtypeStruct((batch, SSM_WIDTH, D_STATE), F32),
                   jax.ShapeDtypeStruct((CONV_WIDTH - 1, batch, CONV_DIM), F32)],
        compiler_params=_params("parallel"),
        name="ssd_sample",
    )(proj, proj, proj, proj, conv0, h0, conv_w, conv_b, dtb_e, alog_e, dskip_e, gate_norm, expand)


def _swa_sample_kernel(q_ref, kn_ref, vn_ref, kc_ref, vc_ref, sink_ref, qcos_ref, qsa_ref,
                       qsb_ref, kcos_ref, ksa_ref, ksb_ref, o_ref, ko_ref, vo_ref):
    w = WINDOW
    nt = DEC_T
    nq = KV_HEADS * nt * (ATT_HEADS // KV_HEADS)
    def token_of(rows):
        return jnp.bitwise_and(jnp.right_shift(rows, 3), nt - 1)

    row_t = token_of(lax.broadcasted_iota(jnp.int32, (nq, w), 0))
    col = lax.broadcasted_iota(jnp.int32, (nq, w), 1)
    mask_c = col > row_t
    col_n = lax.broadcasted_iota(jnp.int32, (nq, SUBLANES), 1)
    row_n = token_of(lax.broadcasted_iota(jnp.int32, (nq, SUBLANES), 0))
    mask_n = col_n <= row_n
    sink = sink_ref[...][:, 0:1]

    def body(i, carry):
        q = (_rope(q_ref[i], qcos_ref[...], qsa_ref[...], qsb_ref[...])
             * (HEAD_DIM ** -0.5)).astype(BF16)
        kn8 = _rope(kn_ref[i], kcos_ref[...], ksa_ref[...], ksb_ref[...])
        vn8 = vn_ref[i]
        kn = kn8[0:nt]
        vn = vn8[0:nt]
        kc = kc_ref[i]
        vc = vc_ref[i]
        s_c = jnp.where(mask_c, _dot_nt(q, kc.astype(BF16)), NEG_INF)
        s_n = jnp.where(mask_n, _dot_nt(q, kn8.astype(BF16)), NEG_INF)
        m = jnp.maximum(jnp.maximum(jnp.max(s_c, axis=-1, keepdims=True),
                                    jnp.max(s_n, axis=-1, keepdims=True)), sink)
        e_c = jnp.exp(s_c - m)
        e_n = jnp.exp(s_n - m)
        den = (jnp.sum(e_c, axis=-1, keepdims=True) + jnp.sum(e_n, axis=-1, keepdims=True)
               + jnp.exp(sink - m))
        o = _dot((e_c / den).astype(BF16), vc.astype(BF16)) + _dot((e_n / den).astype(BF16),
                                                                    vn8.astype(BF16))
        o_ref[i] = o.astype(o_ref.dtype)
        ko_ref[i, 0:w - nt, :] = kc_ref[i, nt:w, :]
        ko_ref[i, w - nt:w, :] = kn
        vo_ref[i, 0:w - nt, :] = vc_ref[i, nt:w, :]
        vo_ref[i, w - nt:w, :] = vn
        return carry

    lax.fori_loop(0, q_ref.shape[0], body, 0)


def swa_sample(q, k_new, v_new, k_cache, v_cache, sink_rows, qtabs, ktabs, *, batch):
    sb = SEQ_TILE
    nq = q.shape[1]
    const = lambda i: (0, 0)
    seq3 = lambda i: (i, 0, 0)
    return pl.pallas_call(
        _swa_sample_kernel,
        grid=(batch // sb,),
        in_specs=[pl.BlockSpec((sb, nq, KV_WIDTH), seq3),
                  pl.BlockSpec((sb, SUBLANES, KV_WIDTH), seq3),
                  pl.BlockSpec((sb, SUBLANES, KV_WIDTH), seq3),
                  pl.BlockSpec((sb, WINDOW, KV_WIDTH), seq3),
                  pl.BlockSpec((sb, WINDOW, KV_WIDTH), seq3),
                  pl.BlockSpec((nq, LANES), const),
                  pl.BlockSpec((nq, LANES), const),
                  pl.BlockSpec((nq, LANES), const),
                  pl.BlockSpec((nq, LANES), const),
                  pl.BlockSpec((SUBLANES, LANES), const),
                  pl.BlockSpec((SUBLANES, LANES), const),
                  pl.BlockSpec((SUBLANES, LANES), const)],
        out_specs=[pl.BlockSpec((sb, nq, KV_WIDTH), seq3),
                   pl.BlockSpec((sb, WINDOW, KV_WIDTH), seq3),
                   pl.BlockSpec((sb, WINDOW, KV_WIDTH), seq3)],
        out_shape=[jax.ShapeDtypeStruct((batch, nq, KV_WIDTH), F32),
                   jax.ShapeDtypeStruct((batch, WINDOW, KV_WIDTH), F32),
                   jax.ShapeDtypeStruct((batch, WINDOW, KV_WIDTH), F32)],
        compiler_params=_params("parallel"),
        name="swa_sample",
    )(q, k_new, v_new, k_cache, v_cache, sink_rows, *qtabs, *ktabs)


def _xattn_sample_kernel(q_ref, mk_ref, mv_ref, o_ref):
    rows = X_HEADS * SUBLANES
    row_head = jnp.right_shift(lax.broadcasted_iota(jnp.int32, (rows, N_MEM * X_HEADS), 0), 3)
    col_head = jnp.bitwise_and(lax.broadcasted_iota(jnp.int32, (rows, N_MEM * X_HEADS), 1),
                               X_HEADS - 1)
    own_head = row_head == col_head

    def body(i, carry):
        q8 = q_ref[i]
        qh = jnp.concatenate([q8[:, h * X_HEAD_DIM:(h + 1) * X_HEAD_DIM] for h in range(X_HEADS)],
                             axis=0).astype(BF16)
        s = _dot_nt(qh, mk_ref[i].astype(BF16)) * (X_HEAD_DIM ** -0.5)
        p = _softmax(jnp.where(own_head, s, NEG_INF))
        o_ref[i] = _dot(p.astype(BF16), mv_ref[i].astype(BF16))
        return carry

    lax.fori_loop(0, q_ref.shape[0], body, 0)


def xattn_sample(q, mk, mv, *, batch):
    sb = SEQ_TILE
    seq3 = lambda i: (i, 0, 0)
    rows = X_HEADS * SUBLANES
    return pl.pallas_call(
        _xattn_sample_kernel,
        grid=(batch // sb,),
        in_specs=[pl.BlockSpec((sb, SUBLANES, X_WIDTH), seq3),
                  pl.BlockSpec((sb, N_MEM * X_HEADS, X_HEAD_DIM), seq3),
                  pl.BlockSpec((sb, N_MEM * X_HEADS, X_HEAD_DIM), seq3)],
        out_specs=pl.BlockSpec((sb, rows, X_HEAD_DIM), seq3),
        out_shape=jax.ShapeDtypeStruct((batch, rows, X_HEAD_DIM), F32),
        compiler_params=_params("parallel"),
        name="xattn_sample",
    )(q, mk, mv)


def _matmul_residual_kernel(a_ref, w_ref, x_ref, o_ref):
    o_ref[...] = x_ref[...] + _dot(a_ref[...], w_ref[...])


def matmul_residual(a, w, x, *, tm):
    m, k = a.shape
    n = w.shape[1]
    return pl.pallas_call(
        _matmul_residual_kernel,
        grid=(m // tm,),
        in_specs=[pl.BlockSpec((tm, k), lambda i: (i, 0)),
                  pl.BlockSpec((k, n), lambda i: (0, 0)),
                  pl.BlockSpec((tm, n), lambda i: (i, 0))],
        out_specs=pl.BlockSpec((tm, n), lambda i: (i, 0)),
        out_shape=jax.ShapeDtypeStruct((m, n), F32),
        compiler_params=_params("parallel"),
        name="matmul_residual",
    )(a, w, x)


def _rope_tables(pos, reps):
    half = ROT_DIM // 2
    inv = ROPE_THETA ** (-np.arange(half, dtype=np.float64) * (2.0 / ROT_DIM))
    ang = np.asarray(pos, np.float64)[:, None] * inv[None, :]
    cos, sin = np.cos(ang), np.sin(ang)
    n = ang.shape[0]
    ones = np.ones((n, HEAD_DIM - ROT_DIM))
    zeros = np.zeros((n, HEAD_DIM - ROT_DIM))
    z8 = np.zeros((n, half))
    c = np.concatenate([cos, cos, ones], axis=1)
    sa = np.concatenate([-sin, z8, zeros], axis=1)
    sb = np.concatenate([z8, sin, zeros], axis=1)
    return tuple(np.repeat(np.tile(t, (1, LANES // HEAD_DIM)), reps, axis=0).astype(np.float32)
                 for t in (c, sa, sb))


def _permute_w_in(w_in):
    s1 = SSM_WIDTH
    s2 = s1 + CONV_DIM
    s3 = s2 + SSM_HEADS
    s4 = s3 + ATT_WIDTH
    cols = [w_in[:, 0:s1],
            w_in[:, s1:s1 + SSM_WIDTH],
            w_in[:, s3:s4],
            w_in[:, s1 + SSM_WIDTH:s2],
            w_in[:, s4:s4 + 2 * KV_WIDTH],
            w_in[:, s2:s3]]
    w = jnp.concatenate(cols, axis=1)
    return jnp.pad(w, ((0, 0), (0, PROJ_WIDTH - w.shape[1]))).astype(BF16)


def _pad_lanes(v, width=LANES):
    return jnp.pad(v, (0, width - v.shape[0]))[None, :]


def kernel(x_prompt, x_sample, mem_prompt, state_ssm, state_conv, cache_swa_k, cache_swa_v,
           cache_mem_k, cache_mem_v, norm_mix, w_in, conv_w, conv_b, dt_bias, a_log, d_skip,
           gate_norm, sinks, w_out, norm_mem, norm_x, w_xq, w_xk, w_xv, w_xo, norm_ffn,
           w_gate, w_up, w_down, norm_final):
    bp, tp, _ = x_prompt.shape
    bs, ts, _ = x_sample.shape
    assert ts == DEC_T and tp % CHUNK == 0 and bs % SEQ_TILE == 0

    w_in_p = _permute_w_in(w_in[0])
    w_out_b = w_out[0].astype(BF16)
    w_xq_b = w_xq[0].astype(BF16)
    w_xkv_b = jnp.concatenate([w_xk[0], w_xv[0]], axis=1).astype(BF16)
    w_xo_b = w_xo[0].astype(BF16)
    w_gate_b, w_up_b, w_down_b = (w[0].astype(BF16) for w in (w_gate, w_up, w_down))
    row = lambda v: v.reshape(1, -1)
    cw = jnp.concatenate([conv_w[0][:, :SSM_WIDTH], conv_w[0][:, SSM_WIDTH:]], axis=1)
    cbias = row(conv_b[0])
    dtb = _pad_lanes(dt_bias[0])
    alog = _pad_lanes(a_log[0])
    dskip_e = row(jnp.repeat(d_skip[0], SSM_HEADDIM))
    dtb_e = row(jnp.repeat(dt_bias[0], SSM_HEADDIM))
    alog_e = row(jnp.repeat(a_log[0], SSM_HEADDIM))
    gn = row(gate_norm[0])
    tril = jnp.asarray(np.tril(np.ones((CHUNK, CHUNK), np.float32)), BF16)
    expand = jnp.asarray(np.arange(LANES)[:, None] == (np.arange(SSM_WIDTH)[None, :] // SSM_HEADDIM),
                         BF16)

    xp = x_prompt.reshape(bp * tp, D_MODEL)
    proj_p = norm_matmul(xp, row(norm_mix[0]), w_in_p, tm=1024, tn=512)
    y_p, p_ssm = ssd_prompt(proj_p, cw, cbias, dtb, alog, dskip_e, gn, tril, expand,
                            batch=bp, seq=tp)
    cos_p, sa_p, sb_p = _rope_tables(np.arange(tp), 1)
    att_p, p_k = swa_prompt(proj_p, sinks[0], cos_p, sa_p, sb_p, batch=bp, seq=tp)
    x1_p, qx_p = out_proj(y_p, att_p, xp, w_out_b, row(norm_x[0]), w_xq_b, tm=512)
    mkv = norm_matmul(mem_prompt.reshape(bp * N_MEM, D_MODEL), row(norm_mem[0]), w_xkv_b,
                      tm=bp * N_MEM, tn=512)
    x2_p = xattn_prompt(qx_p, mkv, x1_p, w_xo_b, batch=bp, seq=tp, tq=512)
    y_prompt = ffn(x2_p, row(norm_ffn[0]), w_gate_b, w_up_b, w_down_b, row(norm_final),
                   tm=512, tf=512).reshape(bp, tp, D_MODEL)

    proj_p3 = proj_p.reshape(bp, tp, PROJ_WIDTH)
    tail = proj_p3[:, tp - (CONV_WIDTH - 1):, :]
    p_conv = jnp.concatenate([tail[:, :, COL_X:COL_X + SSM_WIDTH],
                              tail[:, :, COL_BC:COL_BC + BC_WIDTH]], axis=-1)
    p_v = proj_p3[:, tp - WINDOW:, COL_KV + KV_WIDTH:COL_KV + 2 * KV_WIDTH]
    p_mk = mkv[:, :X_WIDTH].reshape(bp, N_MEM, X_HEADS, X_HEAD_DIM)
    p_mv = mkv[:, X_WIDTH:].reshape(bp, N_MEM, X_HEADS, X_HEAD_DIM)

    xs = x_sample.transpose(1, 0, 2).reshape(ts * bs, D_MODEL)
    proj_s = norm_matmul(xs, row(norm_mix[0]), w_in_p, tm=ts * bs, tn=512)
    proj_s3 = proj_s.reshape(ts, bs, PROJ_WIDTH)
    y_s, s_ssm, s_conv = ssd_sample(
        proj_s3, state_conv[0].transpose(1, 0, 2),
        state_ssm[0].reshape(bs, SSM_WIDTH, D_STATE), cw, cbias, dtb_e, alog_e, dskip_e, gn,
        expand, batch=bs)

    rep = ATT_HEADS // KV_HEADS
    q_s = proj_s3[:, :, COL_Q:COL_Q + ATT_WIDTH].reshape(ts, bs, KV_HEADS, rep, HEAD_DIM)
    q_s = q_s.transpose(1, 2, 0, 3, 4)
    zq = jnp.zeros_like(q_s[:, 0])
    q_bd = jnp.stack([jnp.concatenate([q_s[:, 0], zq], axis=-1),
                      jnp.concatenate([zq, q_s[:, 1]], axis=-1)], axis=1)
    q_bd = q_bd.reshape(bs, KV_HEADS * ts * rep, KV_WIDTH)
    kv_s = proj_s3[:, :, COL_KV:COL_KV + 2 * KV_WIDTH].transpose(1, 0, 2)
    kv_s = jnp.pad(kv_s, ((0, 0), (0, SUBLANES - ts), (0, 0)))
    pos_s = PAST_LEN + np.arange(ts)
    ktabs = _rope_tables(PAST_LEN + np.arange(SUBLANES), 1)
    qtabs = tuple(np.tile(t, (KV_HEADS, 1)) for t in _rope_tables(pos_s, rep))
    sink_rows = jnp.broadcast_to(
        jnp.broadcast_to(sinks[0].reshape(KV_HEADS, 1, rep), (KV_HEADS, ts, rep)).reshape(-1, 1),
        (KV_HEADS * ts * rep, LANES))
    o_bd, s_k, s_v = swa_sample(
        q_bd, kv_s[:, :, :KV_WIDTH], kv_s[:, :, KV_WIDTH:],
        cache_swa_k[0].reshape(bs, WINDOW, KV_WIDTH), cache_swa_v[0].reshape(bs, WINDOW, KV_WIDTH),
        sink_rows, qtabs, ktabs, batch=bs)
    o_bd = o_bd.reshape(bs, KV_HEADS, ts, rep, KV_HEADS, HEAD_DIM)
    att_s = jnp.stack([o_bd[:, g, :, :, g, :] for g in range(KV_HEADS)], axis=2)
    att_s = att_s.transpose(1, 0, 2, 3, 4).reshape(ts * bs, ATT_WIDTH).astype(BF16)

    x1_s, qx_s = out_proj(y_s.reshape(ts * bs, SSM_WIDTH), att_s, xs, w_out_b, row(norm_x[0]),
                          w_xq_b, tm=ts * bs)
    qx_s8 = jnp.pad(qx_s.astype(F32).reshape(ts, bs, X_WIDTH).transpose(1, 0, 2),
                    ((0, 0), (0, SUBLANES - ts), (0, 0)))
    o_x = xattn_sample(qx_s8, cache_mem_k.reshape(bs, N_MEM * X_HEADS, X_HEAD_DIM),
                       cache_mem_v.reshape(bs, N_MEM * X_HEADS, X_HEAD_DIM), batch=bs)
    o_x = o_x.reshape(bs, X_HEADS, SUBLANES, X_HEAD_DIM)[:, :, :ts]
    o_x = o_x.transpose(2, 0, 1, 3).reshape(ts * bs, X_WIDTH).astype(BF16)
    x2_s = matmul_residual(o_x, w_xo_b, x1_s, tm=ts * bs)
    y_s_out = ffn(x2_s, row(norm_ffn[0]), w_gate_b, w_up_b, w_down_b, row(norm_final),
                  tm=ts * bs, tf=512)
    y_sample = y_s_out.reshape(ts, bs, D_MODEL).transpose(1, 0, 2)

    return (y_prompt, y_sample,
            p_ssm.reshape(1, bp, SSM_HEADS, SSM_HEADDIM, D_STATE), p_conv[None],
            p_k.reshape(1, bp, WINDOW, KV_HEADS, HEAD_DIM),
            p_v.reshape(1, bp, WINDOW, KV_HEADS, HEAD_DIM),
            p_mk[None], p_mv[None],
            s_ssm.reshape(1, bs, SSM_HEADS, SSM_HEADDIM, D_STATE),
            s_conv.transpose(1, 0, 2)[None],
            s_k.reshape(1, bs, WINDOW, KV_HEADS, HEAD_DIM),
            s_v.reshape(1, bs, WINDOW, KV_HEADS, HEAD_DIM))
```

```python
import functools

import jax
import jax.numpy as jnp
import numpy as np
from jax import lax
from jax.experimental import pallas as pl
from jax.experimental.pallas import tpu as pltpu

F32 = jnp.float32
BF16 = jnp.bfloat16

D_MODEL = 2048
SSM_WIDTH = 1024
SSM_HEADDIM = 64
SSM_HEADS = 16
SSM_GROUPS = 2
D_STATE = 128
CONV_WIDTH = 4
CONV_DIM = SSM_WIDTH + 2 * SSM_GROUPS * D_STATE
BC_WIDTH = 2 * SSM_GROUPS * D_STATE
ATT_WIDTH = 1024
HEAD_DIM = 64
ATT_HEADS = 16
KV_HEADS = 2
KV_WIDTH = KV_HEADS * HEAD_DIM
WINDOW = 128
CHUNK = 128
ROT_DIM = 16
ROPE_THETA = 500000.0
PAST_LEN = 16384
N_MEM = 256
X_HEADS = 4
X_HEAD_DIM = 128
X_WIDTH = X_HEADS * X_HEAD_DIM
EPS = 1e-5

LANES = 128
SUBLANES = 8
VMEM_LIMIT_BYTES = 56 * 1024 * 1024

PROJ_WIDTH = 4096
COL_Z = 0
COL_X = 1024
COL_Q = 2048
COL_BC = 3072
COL_KV = 3584
COL_DT = 3840

NEG_INF = float("-inf")


def _params(*semantics):
    return pltpu.CompilerParams(dimension_semantics=semantics,
                                vmem_limit_bytes=VMEM_LIMIT_BYTES)


def _rmsnorm(x, gain):
    ms = jnp.mean(x * x, axis=-1, keepdims=True)
    return x * lax.rsqrt(ms + EPS) * gain


def _silu(x):
    return x * jax.nn.sigmoid(x)


def _softplus(x):
    return jnp.maximum(x, 0.0) + jnp.log1p(jnp.exp(-jnp.abs(x)))


def _split_bf16(v, parts):
    out = []
    rem = v
    for _ in range(parts - 1):
        p = rem.astype(BF16)
        out.append(p)
        rem = rem - p.astype(F32)
    out.append(rem.astype(BF16))
    return out


def _dot(a, b):
    return jnp.dot(a, b, preferred_element_type=F32)


def _dot_nt(a, b):
    return lax.dot_general(a, b, (((1,), (1,)), ((), ())), preferred_element_type=F32)


def _dot_tn(a, b):
    return lax.dot_general(a, b, (((0,), (0,)), ((), ())), preferred_element_type=F32)


def _select_matmul(v, sel, parts):
    acc = None
    for p in _split_bf16(v, parts):
        t = _dot(p, sel)
        acc = t if acc is None else acc + t
    return acc


def _norm_matmul_kernel(x_ref, g_ref, w_ref, o_ref, *, tn):
    h = _rmsnorm(x_ref[...], g_ref[...]).astype(BF16)
    for j in range(o_ref.shape[1] // tn):
        o_ref[:, j * tn:(j + 1) * tn] = _dot(h, w_ref[:, j * tn:(j + 1) * tn]).astype(o_ref.dtype)


def norm_matmul(x, gain, w, *, tm, tn, out_dtype=F32):
    m, k = x.shape
    n = w.shape[1]
    return pl.pallas_call(
        functools.partial(_norm_matmul_kernel, tn=tn),
        grid=(m // tm,),
        in_specs=[pl.BlockSpec((tm, k), lambda i: (i, 0)),
                  pl.BlockSpec((1, k), lambda i: (0, 0)),
                  pl.BlockSpec((k, n), lambda i: (0, 0), pipeline_mode=pl.Buffered(1))],
        out_specs=pl.BlockSpec((tm, n), lambda i: (i, 0)),
        out_shape=jax.ShapeDtypeStruct((m, n), out_dtype),
        compiler_params=_params("parallel"),
        name="norm_matmul",
    )(x, gain, w)


def _ssd_prompt_kernel(z_ref, x_ref, bc_ref, dt_ref, cw_ref, cb_ref, dtb_ref, alog_ref,
                       dskip_ref, gn_ref, tril_ref, exp_ref,
                       y_ref, state_ref, xpad_ref, st_ref):
    c = pl.program_id(1)
    nc = pl.num_programs(1)
    t = CHUNK

    @pl.when(c == 0)
    def _():
        xpad_ref[0:SUBLANES, :] = jnp.zeros((SUBLANES, CONV_DIM), F32)
        st_ref[...] = jnp.zeros_like(st_ref)

    xpad_ref[SUBLANES:SUBLANES + t, 0:SSM_WIDTH] = x_ref[...]
    xpad_ref[SUBLANES:SUBLANES + t, SSM_WIDTH:CONV_DIM] = bc_ref[...]
    acc = jnp.broadcast_to(cb_ref[...], (t, CONV_DIM))
    for k in range(CONV_WIDTH):
        lo = SUBLANES - (CONV_WIDTH - 1) + k
        acc = acc + xpad_ref[lo:lo + t, :] * cw_ref[k:k + 1, :]
    xbc = _silu(acc)
    xpad_ref[0:SUBLANES, :] = xpad_ref[t:t + SUBLANES, :]

    xs = xbc[:, 0:SSM_WIDTH]
    xs_b = xs.astype(BF16)
    b_mat = [xbc[:, SSM_WIDTH + g * D_STATE:SSM_WIDTH + (g + 1) * D_STATE].astype(BF16)
             for g in range(SSM_GROUPS)]
    c_off = SSM_WIDTH + SSM_GROUPS * D_STATE
    c_mat = [xbc[:, c_off + g * D_STATE:c_off + (g + 1) * D_STATE].astype(BF16)
             for g in range(SSM_GROUPS)]

    dt = _softplus(dt_ref[...] + dtb_ref[...])
    a = -jnp.exp(alog_ref[...])
    da = dt * a
    cs = _select_matmul_left(tril_ref[...], da)
    cs_t = cs.T
    dt_t = dt.T
    cs_last = cs[t - 1:t, :]
    exp_cs = jnp.exp(cs)
    to_end = jnp.exp(cs_last - cs) * dt
    sel = exp_ref[...]
    exp_cs_e = _select_matmul(exp_cs, sel, 2)
    to_end_e = _select_matmul(to_end, sel, 2)

    row = lax.broadcasted_iota(jnp.int32, (t, 2 * t), 0)
    col = lax.broadcasted_iota(jnp.int32, (t, 2 * t), 1)
    causal = jnp.bitwise_and(col, t - 1) <= row
    lane = lax.broadcasted_iota(jnp.int32, (t, LANES), 1)
    left = lane < SSM_HEADDIM

    cb = [_dot_nt(c_mat[g], b_mat[g]) for g in range(SSM_GROUPS)]
    cb2 = [jnp.concatenate([m, m], axis=1) for m in cb]

    y_parts = []
    for j in range(SSM_HEADS // 2):
        g = (2 * j) // (SSM_HEADS // SSM_GROUPS)
        h0, h1 = 2 * j, 2 * j + 1
        colv = jnp.concatenate([jnp.broadcast_to(cs[:, h0:h0 + 1], (t, t)),
                                jnp.broadcast_to(cs[:, h1:h1 + 1], (t, t))], axis=1)
        rowv = jnp.concatenate([jnp.broadcast_to(cs_t[h0:h0 + 1, :], (t, t)),
                                jnp.broadcast_to(cs_t[h1:h1 + 1, :], (t, t))], axis=1)
        dtr = jnp.concatenate([jnp.broadcast_to(dt_t[h0:h0 + 1, :], (t, t)),
                               jnp.broadcast_to(dt_t[h1:h1 + 1, :], (t, t))], axis=1)
        decay = jnp.where(causal, jnp.exp(colv - rowv), 0.0)
        wts = (cb2[g] * decay * dtr).astype(BF16)
        xp = xs_b[:, j * LANES:(j + 1) * LANES]
        zero = jnp.zeros_like(xp)
        rhs = jnp.concatenate([jnp.where(left, xp, zero), jnp.where(left, zero, xp)], axis=0)
        y_parts.append(_dot(wts, rhs))
    y_diag = jnp.concatenate(y_parts, axis=1)

    half = SSM_WIDTH // SSM_GROUPS
    st = st_ref[...]
    st_b = st.astype(BF16)
    y_off = jnp.concatenate(
        [_dot(c_mat[g], st_b[:, g * half:(g + 1) * half]) for g in range(SSM_GROUPS)], axis=1)
    xw = (xs * to_end_e).astype(BF16)
    upd = jnp.concatenate(
        [_dot_tn(b_mat[g], xw[:, g * half:(g + 1) * half]) for g in range(SSM_GROUPS)], axis=1)
    st_new = st * exp_cs_e[t - 1:t, :] + upd
    st_ref[...] = st_new

    y = y_diag + y_off * exp_cs_e + xs * dskip_ref[...]
    gated = y * _silu(z_ref[...])
    outs = []
    for g in range(SSM_GROUPS):
        outs.append(_rmsnorm(gated[:, g * half:(g + 1) * half], gn_ref[:, g * half:(g + 1) * half]))
    y_ref[...] = jnp.concatenate(outs, axis=1).astype(y_ref.dtype)

    @pl.when(c == nc - 1)
    def _():
        state_ref[0] = st_new.T


def _select_matmul_left(sel, v):
    acc = None
    for p in _split_bf16(v, 3):
        t = _dot(sel, p)
        acc = t if acc is None else acc + t
    return acc


def ssd_prompt(proj, conv_w, conv_b, dtb, alog, dskip_e, gate_norm, tril, expand, *, batch, seq):
    nc = seq // CHUNK
    rows = lambda b, c: b * nc + c
    const = lambda b, c: (0, 0)
    return pl.pallas_call(
        _ssd_prompt_kernel,
        grid=(batch, nc),
        in_specs=[
            pl.BlockSpec((CHUNK, SSM_WIDTH), lambda b, c: (rows(b, c), COL_Z // SSM_WIDTH)),
            pl.BlockSpec((CHUNK, SSM_WIDTH), lambda b, c: (rows(b, c), COL_X // SSM_WIDTH)),
            pl.BlockSpec((CHUNK, BC_WIDTH), lambda b, c: (rows(b, c), COL_BC // BC_WIDTH)),
            pl.BlockSpec((CHUNK, LANES), lambda b, c: (rows(b, c), COL_DT // LANES)),
            pl.BlockSpec((CONV_WIDTH, CONV_DIM), const),
            pl.BlockSpec((1, CONV_DIM), const),
            pl.BlockSpec((1, LANES), const),
            pl.BlockSpec((1, LANES), const),
            pl.BlockSpec((1, SSM_WIDTH), const),
            pl.BlockSpec((1, SSM_WIDTH), const),
            pl.BlockSpec((CHUNK, CHUNK), const),
            pl.BlockSpec((LANES, SSM_WIDTH), const),
        ],
        out_specs=[pl.BlockSpec((CHUNK, SSM_WIDTH), lambda b, c: (rows(b, c), 0)),
                   pl.BlockSpec((1, SSM_WIDTH, D_STATE), lambda b, c: (b, 0, 0))],
        out_shape=[jax.ShapeDtypeStruct((batch * seq, SSM_WIDTH), BF16),
                   jax.ShapeDtypeStruct((batch, SSM_WIDTH, D_STATE), F32)],
        scratch_shapes=[pltpu.VMEM((CHUNK + 2 * SUBLANES, CONV_DIM), F32),
                        pltpu.VMEM((D_STATE, SSM_WIDTH), F32)],
        compiler_params=_params("parallel", "arbitrary"),
        name="ssd_prompt",
    )(proj, proj, proj, proj, conv_w, conv_b, dtb, alog, dskip_e, gate_norm, tril, expand)


def _rope(x, cos, sin_a, sin_b):
    parts = []
    for i in range(x.shape[1] // LANES):
        xb = x[:, i * LANES:(i + 1) * LANES]
        up = pltpu.roll(xb, LANES - ROT_DIM // 2, axis=1)
        dn = pltpu.roll(xb, ROT_DIM // 2, axis=1)
        parts.append(xb * cos + up * sin_a + dn * sin_b)
    return parts[0] if len(parts) == 1 else jnp.concatenate(parts, axis=1)


def _block_diag_rows(a, a_swapped, first):
    lane = lax.broadcasted_iota(jnp.int32, a.shape, 1)
    left = lane < HEAD_DIM
    zero = jnp.zeros_like(a)
    if first == 0:
        top, bot = jnp.where(left, a, zero), jnp.where(left, zero, a_swapped)
    else:
        top, bot = jnp.where(left, a_swapped, zero), jnp.where(left, zero, a)
    return jnp.concatenate([top, bot], axis=0)


def _swa_prompt_kernel(sink_ref, q_ref, kv_ref, cos_ref, sa_ref, sb_ref,
                       att_ref, k_ref, kprev_ref, vprev_ref, s_ref, p_ref):
    blk = pl.program_id(1)
    w = WINDOW
    pairs = ATT_HEADS // 2

    @pl.when(blk == 0)
    def _():
        kprev_ref[...] = jnp.zeros_like(kprev_ref)
        vprev_ref[...] = jnp.zeros_like(vprev_ref)

    cos, sa, sb = cos_ref[...], sa_ref[...], sb_ref[...]
    k_cur = _rope(kv_ref[:, 0:KV_WIDTH], cos, sa, sb)
    v_cur = kv_ref[:, KV_WIDTH:2 * KV_WIDTH]
    k_ref[0] = k_cur
    q = (_rope(q_ref[...], cos, sa, sb) * (HEAD_DIM ** -0.5)).astype(BF16)

    k_cat = jnp.concatenate([kprev_ref[...], k_cur], axis=0)
    v_cat = jnp.concatenate([vprev_ref[...], v_cur], axis=0)
    k_sw = pltpu.roll(k_cat, HEAD_DIM, axis=1)
    v_sw = pltpu.roll(v_cat, HEAD_DIM, axis=1)
    kbd = [_block_diag_rows(k_cat, k_sw, g).astype(BF16) for g in range(KV_HEADS)]
    vbd_t = [_block_diag_rows(v_cat, v_sw, g).T.astype(BF16) for g in range(KV_HEADS)]

    for j in range(pairs):
        g = (2 * j) // (ATT_HEADS // KV_HEADS)
        s_ref[j] = _dot_nt(kbd[g], q[:, j * LANES:(j + 1) * LANES])

    key = lax.broadcasted_iota(jnp.int32, (w, w), 0)
    qry = lax.broadcasted_iota(jnp.int32, (w, w), 1)
    own_valid = key <= qry
    has_prev = blk > 0
    for j in range(pairs):
        for hh in range(2):
            base = hh * 2 * w
            sink = sink_ref[2 * j + hh]
            prev = jnp.where(has_prev, s_ref[j, base:base + w, :], NEG_INF)
            s = jnp.where(own_valid, s_ref[j, base + w:base + 2 * w, :], prev)
            m = jnp.maximum(jnp.max(s, axis=0, keepdims=True), sink)
            e = jnp.exp(s - m)
            den = jnp.sum(e, axis=0, keepdims=True) + jnp.exp(sink - m)
            p = e * (1.0 / den)
            p_ref[j, base:base + w, :] = jnp.where(own_valid, 0.0, p).astype(BF16)
            p_ref[j, base + w:base + 2 * w, :] = jnp.where(own_valid, p, 0.0).astype(BF16)

    for j in range(pairs):
        g = (2 * j) // (ATT_HEADS // KV_HEADS)
        o_t = _dot(vbd_t[g], p_ref[j])
        att_ref[:, j * LANES:(j + 1) * LANES] = o_t.T.astype(att_ref.dtype)

    kprev_ref[...] = k_cur
    vprev_ref[...] = v_cur


def swa_prompt(proj, sinks, cos, sin_a, sin_b, *, batch, seq):
    nb = seq // WINDOW
    rows = lambda b, i: b * nb + i
    return pl.pallas_call(
        _swa_prompt_kernel,
        grid=(batch, nb),
        in_specs=[
            pl.BlockSpec(memory_space=pltpu.SMEM),
            pl.BlockSpec((WINDOW, ATT_WIDTH), lambda b, i: (rows(b, i), COL_Q // ATT_WIDTH)),
            pl.BlockSpec((WINDOW, 2 * KV_WIDTH), lambda b, i: (rows(b, i), COL_KV // (2 * KV_WIDTH))),
            pl.BlockSpec((WINDOW, LANES), lambda b, i: (i, 0)),
            pl.BlockSpec((WINDOW, LANES), lambda b, i: (i, 0)),
            pl.BlockSpec((WINDOW, LANES), lambda b, i: (i, 0)),
        ],
        out_specs=[pl.BlockSpec((WINDOW, ATT_WIDTH), lambda b, i: (rows(b, i), 0)),
                   pl.BlockSpec((1, WINDOW, KV_WIDTH), lambda b, i: (b, 0, 0))],
        out_shape=[jax.ShapeDtypeStruct((batch * seq, ATT_WIDTH), BF16),
                   jax.ShapeDtypeStruct((batch, WINDOW, KV_WIDTH), F32)],
        scratch_shapes=[pltpu.VMEM((WINDOW, KV_WIDTH), F32),
                        pltpu.VMEM((WINDOW, KV_WIDTH), F32),
                        pltpu.VMEM((ATT_HEADS // 2, 4 * WINDOW, WINDOW), F32),
                        pltpu.VMEM((ATT_HEADS // 2, 4 * WINDOW, WINDOW), BF16)],
        compiler_params=_params("parallel", "arbitrary"),
        name="swa_prompt",
    )(sinks, proj, proj, cos, sin_a, sin_b)


def _out_proj_kernel(y_ref, a_ref, x_ref, wo_ref, gx_ref, wq_ref, x1_ref, q_ref):
    mix = _dot(y_ref[...], wo_ref[0:SSM_WIDTH, :]) + _dot(a_ref[...], wo_ref[SSM_WIDTH:, :])
    x1 = x_ref[...] + mix
    x1_ref[...] = x1
    h = _rmsnorm(x1, gx_ref[...]).astype(BF16)
    q_ref[...] = _dot(h, wq_ref[...]).astype(q_ref.dtype)


def out_proj(y, att, x, w_out, norm_x, w_xq, *, tm):
    m = x.shape[0]
    const = lambda i: (0, 0)
    return pl.pallas_call(
        _out_proj_kernel,
        grid=(m // tm,),
        in_specs=[pl.BlockSpec((tm, SSM_WIDTH), lambda i: (i, 0)),
                  pl.BlockSpec((tm, ATT_WIDTH), lambda i: (i, 0)),
                  pl.BlockSpec((tm, D_MODEL), lambda i: (i, 0)),
                  pl.BlockSpec((D_MODEL, D_MODEL), const),
                  pl.BlockSpec((1, D_MODEL), const),
                  pl.BlockSpec((D_MODEL, X_WIDTH), const)],
        out_specs=[pl.BlockSpec((tm, D_MODEL), lambda i: (i, 0)),
                   pl.BlockSpec((tm, X_WIDTH), lambda i: (i, 0))],
        out_shape=[jax.ShapeDtypeStruct((m, D_MODEL), F32),
                   jax.ShapeDtypeStruct((m, X_WIDTH), BF16)],
        compiler_params=_params("parallel"),
        name="out_proj",
    )(y, att, x, w_out, norm_x, w_xq)


def _softmax(s):
    m = jnp.max(s, axis=-1, keepdims=True)
    e = jnp.exp(s - m)
    return e / jnp.sum(e, axis=-1, keepdims=True)


def _xattn_prompt_kernel(q_ref, mk_ref, mv_ref, x1_ref, wo_ref, x2_ref):
    mk = mk_ref[...].astype(BF16)
    mv = mv_ref[...].astype(BF16)
    outs = []
    for h in range(X_HEADS):
        sl = slice(h * X_HEAD_DIM, (h + 1) * X_HEAD_DIM)
        s = _dot_nt(q_ref[:, sl], mk[:, sl]) * (X_HEAD_DIM ** -0.5)
        outs.append(_dot(_softmax(s).astype(BF16), mv[:, sl]).astype(BF16))
    o = jnp.concatenate(outs, axis=1)
    x2_ref[...] = x1_ref[...] + _dot(o, wo_ref[...])


def xattn_prompt(q, mkv, x1, w_xo, *, batch, seq, tq):
    nq = seq // tq
    return pl.pallas_call(
        _xattn_prompt_kernel,
        grid=(batch, nq),
        in_specs=[pl.BlockSpec((tq, X_WIDTH), lambda b, i: (b * nq + i, 0)),
                  pl.BlockSpec((N_MEM, X_WIDTH), lambda b, i: (b, 0)),
                  pl.BlockSpec((N_MEM, X_WIDTH), lambda b, i: (b, 1)),
                  pl.BlockSpec((tq, D_MODEL), lambda b, i: (b * nq + i, 0)),
                  pl.BlockSpec((X_WIDTH, D_MODEL), lambda b, i: (0, 0))],
        out_specs=pl.BlockSpec((tq, D_MODEL), lambda b, i: (b * nq + i, 0)),
        out_shape=jax.ShapeDtypeStruct(x1.shape, F32),
        compiler_params=_params("parallel", "parallel"),
        name="xattn_prompt",
    )(q, mkv, mkv, x1, w_xo)


def _ffn_kernel(x_ref, gn_ref, wg_ref, wu_ref, wd_ref, gf_ref, o_ref, h_ref):
    f = pl.program_id(1)

    @pl.when(f == 0)
    def _():
        x = x_ref[...]
        h_ref[...] = _rmsnorm(x, gn_ref[...]).astype(BF16)
        o_ref[...] = x

    h = h_ref[...]
    act = (_silu(_dot(h, wg_ref[...])) * _dot(h, wu_ref[...])).astype(BF16)
    o_ref[...] += _dot(act, wd_ref[...])

    @pl.when(f == pl.num_programs(1) - 1)
    def _():
        o_ref[...] = _rmsnorm(o_ref[...], gf_ref[...])


def ffn(x, norm_ffn, w_gate, w_up, w_down, norm_final, *, tm, tf):
    m = x.shape[0]
    d_ff = w_gate.shape[1]
    return pl.pallas_call(
        _ffn_kernel,
        grid=(m // tm, d_ff // tf),
        in_specs=[pl.BlockSpec((tm, D_MODEL), lambda i, f: (i, 0)),
                  pl.BlockSpec((1, D_MODEL), lambda i, f: (0, 0)),
                  pl.BlockSpec((D_MODEL, tf), lambda i, f: (0, f)),
                  pl.BlockSpec((D_MODEL, tf), lambda i, f: (0, f)),
                  pl.BlockSpec((tf, D_MODEL), lambda i, f: (f, 0)),
                  pl.BlockSpec((1, D_MODEL), lambda i, f: (0, 0))],
        out_specs=pl.BlockSpec((tm, D_MODEL), lambda i, f: (i, 0)),
        out_shape=jax.ShapeDtypeStruct((m, D_MODEL), F32),
        scratch_shapes=[pltpu.VMEM((tm, D_MODEL), BF16)],
        compiler_params=_params("parallel", "arbitrary"),
        name="ffn",
    )(x, norm_ffn, w_gate, w_up, w_down, norm_final)


SEQ_TILE = SUBLANES
DEC_T = 4


def _ssd_sample_kernel(z_ref, x_ref, bc_ref, dt_ref, conv0_ref, h0_ref, cw_ref, cb_ref,
                       dtbe_ref, aloge_ref, dskip_ref, gn_ref, exp_ref,
                       y_ref, hnew_ref, conv_ref):
    nt, sb = DEC_T, SEQ_TILE
    half = SSM_WIDTH // SSM_GROUPS

    xin = [conv0_ref[j] for j in range(CONV_WIDTH - 1)]
    xin += [jnp.concatenate([x_ref[u], bc_ref[u]], axis=1) for u in range(nt)]
    for j in range(CONV_WIDTH - 1):
        conv_ref[j] = xin[nt + j]
    xbc = []
    for u in range(nt):
        acc = jnp.broadcast_to(cb_ref[...], (sb, CONV_DIM))
        for k in range(CONV_WIDTH):
            acc = acc + xin[u + k] * cw_ref[k:k + 1, :]
        xbc.append(_silu(acc))
    xs = [v[:, 0:SSM_WIDTH] for v in xbc]
    c_off = SSM_WIDTH + SSM_GROUPS * D_STATE
    b_rows = [v[:, SSM_WIDTH:c_off] for v in xbc]
    c_rows = [v[:, c_off:CONV_DIM] for v in xbc]

    dt_raw = jnp.concatenate([dt_ref[u] for u in range(nt)], axis=0)
    dt_e = _softplus(_select_matmul(dt_raw, exp_ref[...], 3) + dtbe_ref[...])
    da_e = dt_e * -jnp.exp(aloge_ref[...])
    dts = [dt_e[u * sb:(u + 1) * sb] for u in range(nt)]
    cs = []
    for u in range(nt):
        d = da_e[u * sb:(u + 1) * sb]
        cs.append(d if u == 0 else cs[-1] + d)

    lane = lax.broadcasted_iota(jnp.int32, (sb, SSM_WIDTH), 1)
    first_group = lane < half

    def group_bcast(v0, v1):
        return jnp.where(first_group, v0, v1)

    y = []
    for u in range(nt):
        acc = None
        for s in range(u + 1):
            prod = c_rows[u] * b_rows[s]
            cbv = [jnp.sum(prod[:, g * D_STATE:(g + 1) * D_STATE], axis=-1, keepdims=True)
                   for g in range(SSM_GROUPS)]
            coef = group_bcast(cbv[0], cbv[1]) * jnp.exp(cs[u] - cs[s]) * dts[s]
            term = coef * xs[s]
            acc = term if acc is None else acc + term
        y.append(acc)

    c_stack = jnp.concatenate(c_rows, axis=0).astype(BF16)
    to_end = [jnp.exp(cs[nt - 1] - cs[u]) * dts[u] for u in range(nt)]
    xw = [xs[u] * to_end[u] for u in range(nt)]
    dec_parts = [p.astype(F32) for p in _split_bf16(jnp.exp(cs[nt - 1]), 3)]
    pad_rows = jnp.zeros((sb, SSM_WIDTH), F32)
    lhs_t = jnp.concatenate(xw + dec_parts + [pad_rows], axis=0).T.astype(BF16)
    ones = jnp.ones((sb, D_STATE), F32)
    zeros = jnp.zeros((sb, D_STATE), F32)
    rhs = []
    for g in range(SSM_GROUPS):
        bg = [v[:, g * D_STATE:(g + 1) * D_STATE] for v in b_rows]
        left = jnp.concatenate(bg + [zeros] * 4, axis=0)
        right = jnp.concatenate([zeros] * nt + [ones] * 3 + [zeros], axis=0)
        rhs.append(jnp.concatenate([left, right], axis=1))
    krow = jnp.bitwise_and(lax.broadcasted_iota(jnp.int32, (2 * nt * sb, 2 * D_STATE), 0), sb - 1)
    yrow = jnp.bitwise_and(lax.broadcasted_iota(jnp.int32, (nt * sb, SSM_WIDTH), 0), sb - 1)

    y_off = jnp.zeros((nt * sb, SSM_WIDTH), F32)
    for i in range(sb):
        h0 = h0_ref[i]
        h0_b = h0.astype(BF16)
        off = jnp.concatenate(
            [_dot_nt(c_stack[:, g * D_STATE:(g + 1) * D_STATE], h0_b[g * half:(g + 1) * half])
             for g in range(SSM_GROUPS)], axis=1)
        y_off = y_off + jnp.where(yrow == i, off, 0.0)
        new = []
        for g in range(SSM_GROUPS):
            r = jnp.where(krow == i, rhs[g], 0.0).astype(BF16)
            res = _dot(lhs_t[g * half:(g + 1) * half], r)
            new.append(h0[g * half:(g + 1) * half] * res[:, D_STATE:] + res[:, :D_STATE])
        hnew_ref[i] = jnp.concatenate(new, axis=0)

    for u in range(nt):
        yu = y[u] + y_off[u * sb:(u + 1) * sb] * jnp.exp(cs[u]) + xs[u] * dskip_ref[...]
        gated = yu * _silu(z_ref[u])
        outs = [_rmsnorm(gated[:, g * half:(g + 1) * half], gn_ref[:, g * half:(g + 1) * half])
                for g in range(SSM_GROUPS)]
        y_ref[u] = jnp.concatenate(outs, axis=1).astype(y_ref.dtype)


def ssd_sample(proj, conv0, h0, conv_w, conv_b, dtb_e, alog_e, dskip_e, gate_norm, expand, *,
               batch):
    sb = SEQ_TILE
    const = lambda i: (0, 0)
    return pl.pallas_call(
        _ssd_sample_kernel,
        grid=(batch // sb,),
        in_specs=[
            pl.BlockSpec((DEC_T, sb, SSM_WIDTH), lambda i: (0, i, COL_Z // SSM_WIDTH)),
            pl.BlockSpec((DEC_T, sb, SSM_WIDTH), lambda i: (0, i, COL_X // SSM_WIDTH)),
            pl.BlockSpec((DEC_T, sb, BC_WIDTH), lambda i: (0, i, COL_BC // BC_WIDTH)),
            pl.BlockSpec((DEC_T, sb, LANES), lambda i: (0, i, COL_DT // LANES)),
            pl.BlockSpec((CONV_WIDTH - 1, sb, CONV_DIM), lambda i: (0, i, 0)),
            pl.BlockSpec((sb, SSM_WIDTH, D_STATE), lambda i: (i, 0, 0)),
            pl.BlockSpec((CONV_WIDTH, CONV_DIM), const),
            pl.BlockSpec((1, CONV_DIM), const),
            pl.BlockSpec((1, SSM_WIDTH), const),
            pl.BlockSpec((1, SSM_WIDTH), const),
            pl.BlockSpec((1, SSM_WIDTH), const),
            pl.BlockSpec((1, SSM_WIDTH), const),
            pl.BlockSpec((LANES, SSM_WIDTH), const),
        ],
        out_specs=[pl.BlockSpec((DEC_T, sb, SSM_WIDTH), lambda i: (0, i, 0)),
                   pl.BlockSpec((sb, SSM_WIDTH, D_STATE), lambda i: (i, 0, 0)),
                   pl.BlockSpec((CONV_WIDTH - 1, sb, CONV_DIM), lambda i: (0, i, 0))],
        out_shape=[jax.ShapeDtypeStruct((DEC_T, batch, SSM_WIDTH), BF16),
                   jax.ShapeDtypeStruct((batch, SSM_WIDTH, D_STATE), F32),
                   jax.ShapeDtypeStruct((CONV_WIDTH - 1, batch, CONV_DIM), F32)],
        compiler_params=_params("parallel"),
        name="ssd_sample",
    )(proj, proj, proj, proj, conv0, h0, conv_w, conv_b, dtb_e, alog_e, dskip_e, gate_norm, expand)


def _swa_sample_kernel(q_ref, kn_ref, vn_ref, kc_ref, vc_ref, sink_ref, qcos_ref, qsa_ref,
                       qsb_ref, kcos_ref, ksa_ref, ksb_ref, o_ref, ko_ref, vo_ref):
    w = WINDOW
    nt = DEC_T
    nq = KV_HEADS * nt * (ATT_HEADS // KV_HEADS)
    def token_of(rows):
        return jnp.bitwise_and(jnp.right_shift(rows, 3), nt - 1)

    row_t = token_of(lax.broadcasted_iota(jnp.int32, (nq, w), 0))
    col = lax.broadcasted_iota(jnp.int32, (nq, w), 1)
    mask_c = col > row_t
    col_n = lax.broadcasted_iota(jnp.int32, (nq, SUBLANES), 1)
    row_n = token_of(lax.broadcasted_iota(jnp.int32, (nq, SUBLANES), 0))
    mask_n = col_n <= row_n
    sink = sink_ref[...][:, 0:1]

    def body(i, carry):
        q = (_rope(q_ref[i], qcos_ref[...], qsa_ref[...], qsb_ref[...])
             * (HEAD_DIM ** -0.5)).astype(BF16)
        kn8 = _rope(kn_ref[i], kcos_ref[...], ksa_ref[...], ksb_ref[...])
        vn8 = vn_ref[i]
        kn = kn8[0:nt]
        vn = vn8[0:nt]
        kc = kc_ref[i]
        vc = vc_ref[i]
        s_c = jnp.where(mask_c, _dot_nt(q, kc.astype(BF16)), NEG_INF)
        s_n = jnp.where(mask_n, _dot_nt(q, kn8.astype(BF16)), NEG_INF)
        m = jnp.maximum(jnp.maximum(jnp.max(s_c, axis=-1, keepdims=True),
                                    jnp.max(s_n, axis=-1, keepdims=True)), sink)
        e_c = jnp.exp(s_c - m)
        e_n = jnp.exp(s_n - m)
        den = (jnp.sum(e_c, axis=-1, keepdims=True) + jnp.sum(e_n, axis=-1, keepdims=True)
               + jnp.exp(sink - m))
        o = _dot((e_c / den).astype(BF16), vc.astype(BF16)) + _dot((e_n / den).astype(BF16),
                                                                    vn8.astype(BF16))
        o_ref[i] = o.astype(o_ref.dtype)
        ko_ref[i, 0:w - nt, :] = kc_ref[i, nt:w, :]
        ko_ref[i, w - nt:w, :] = kn
        vo_ref[i, 0:w - nt, :] = vc_ref[i, nt:w, :]
        vo_ref[i, w - nt:w, :] = vn
        return carry

    lax.fori_loop(0, q_ref.shape[0], body, 0)


def swa_sample(q, k_new, v_new, k_cache, v_cache, sink_rows, qtabs, ktabs, *, batch):
    sb = SEQ_TILE
    nq = q.shape[1]
    const = lambda i: (0, 0)
    seq3 = lambda i: (i, 0, 0)
    return pl.pallas_call(
        _swa_sample_kernel,
        grid=(batch // sb,),
        in_specs=[pl.BlockSpec((sb, nq, KV_WIDTH), seq3),
                  pl.BlockSpec((sb, SUBLANES, KV_WIDTH), seq3),
                  pl.BlockSpec((sb, SUBLANES, KV_WIDTH), seq3),
                  pl.BlockSpec((sb, WINDOW, KV_WIDTH), seq3),
                  pl.BlockSpec((sb, WINDOW, KV_WIDTH), seq3),
                  pl.BlockSpec((nq, LANES), const),
                  pl.BlockSpec((nq, LANES), const),
                  pl.BlockSpec((nq, LANES), const),
                  pl.BlockSpec((nq, LANES), const),
                  pl.BlockSpec((SUBLANES, LANES), const),
                  pl.BlockSpec((SUBLANES, LANES), const),
                  pl.BlockSpec((SUBLANES, LANES), const)],
        out_specs=[pl.BlockSpec((sb, nq, KV_WIDTH), seq3),
                   pl.BlockSpec((sb, WINDOW, KV_WIDTH), seq3),
                   pl.BlockSpec((sb, WINDOW, KV_WIDTH), seq3)],
        out_shape=[jax.ShapeDtypeStruct((batch, nq, KV_WIDTH), F32),
                   jax.ShapeDtypeStruct((batch, WINDOW, KV_WIDTH), F32),
                   jax.ShapeDtypeStruct((batch, WINDOW, KV_WIDTH), F32)],
        compiler_params=_params("parallel"),
        name="swa_sample",
    )(q, k_new, v_new, k_cache, v_cache, sink_rows, *qtabs, *ktabs)


def _xattn_sample_kernel(q_ref, mk_ref, mv_ref, o_ref):
    rows = X_HEADS * SUBLANES
    row_head = jnp.right_shift(lax.broadcasted_iota(jnp.int32, (rows, N_MEM * X_HEADS), 0), 3)
    col_head = jnp.bitwise_and(lax.broadcasted_iota(jnp.int32, (rows, N_MEM * X_HEADS), 1),
                               X_HEADS - 1)
    own_head = row_head == col_head

    def body(i, carry):
        q8 = q_ref[i]
        qh = jnp.concatenate([q8[:, h * X_HEAD_DIM:(h + 1) * X_HEAD_DIM] for h in range(X_HEADS)],
                             axis=0).astype(BF16)
        s = _dot_nt(qh, mk_ref[i].astype(BF16)) * (X_HEAD_DIM ** -0.5)
        p = _softmax(jnp.where(own_head, s, NEG_INF))
        o_ref[i] = _dot(p.astype(BF16), mv_ref[i].astype(BF16))
        return carry

    lax.fori_loop(0, q_ref.shape[0], body, 0)


def xattn_sample(q, mk, mv, *, batch):
    sb = SEQ_TILE
    seq3 = lambda i: (i, 0, 0)
    rows = X_HEADS * SUBLANES
    return pl.pallas_call(
        _xattn_sample_kernel,
        grid=(batch // sb,),
        in_specs=[pl.BlockSpec((sb, SUBLANES, X_WIDTH), seq3),
                  pl.BlockSpec((sb, N_MEM * X_HEADS, X_HEAD_DIM), seq3),
                  pl.BlockSpec((sb, N_MEM * X_HEADS, X_HEAD_DIM), seq3)],
        out_specs=pl.BlockSpec((sb, rows, X_HEAD_DIM), seq3),
        out_shape=jax.ShapeDtypeStruct((batch, rows, X_HEAD_DIM), F32),
        compiler_params=_params("parallel"),
        name="xattn_sample",
    )(q, mk, mv)


def _matmul_residual_kernel(a_ref, w_ref, x_ref, o_ref):
    o_ref[...] = x_ref[...] + _dot(a_ref[...], w_ref[...])


def matmul_residual(a, w, x, *, tm):
    m, k = a.shape
    n = w.shape[1]
    return pl.pallas_call(
        _matmul_residual_kernel,
        grid=(m // tm,),
        in_specs=[pl.BlockSpec((tm, k), lambda i: (i, 0)),
                  pl.BlockSpec((k, n), lambda i: (0, 0)),
                  pl.BlockSpec((tm, n), lambda i: (i, 0))],
        out_specs=pl.BlockSpec((tm, n), lambda i: (i, 0)),
        out_shape=jax.ShapeDtypeStruct((m, n), F32),
        compiler_params=_params("parallel"),
        name="matmul_residual",
    )(a, w, x)


def _rope_tables(pos, reps):
    half = ROT_DIM // 2
    inv = ROPE_THETA ** (-np.arange(half, dtype=np.float64) * (2.0 / ROT_DIM))
    ang = np.asarray(pos, np.float64)[:, None] * inv[None, :]
    cos, sin = np.cos(ang), np.sin(ang)
    n = ang.shape[0]
    ones = np.ones((n, HEAD_DIM - ROT_DIM))
    zeros = np.zeros((n, HEAD_DIM - ROT_DIM))
    z8 = np.zeros((n, half))
    c = np.concatenate([cos, cos, ones], axis=1)
    sa = np.concatenate([-sin, z8, zeros], axis=1)
    sb = np.concatenate([z8, sin, zeros], axis=1)
    return tuple(np.repeat(np.tile(t, (1, LANES // HEAD_DIM)), reps, axis=0).astype(np.float32)
                 for t in (c, sa, sb))


def _permute_w_in(w_in):
    s1 = SSM_WIDTH
    s2 = s1 + CONV_DIM
    s3 = s2 + SSM_HEADS
    s4 = s3 + ATT_WIDTH
    cols = [w_in[:, 0:s1],
            w_in[:, s1:s1 + SSM_WIDTH],
            w_in[:, s3:s4],
            w_in[:, s1 + SSM_WIDTH:s2],
            w_in[:, s4:s4 + 2 * KV_WIDTH],
            w_in[:, s2:s3]]
    w = jnp.concatenate(cols, axis=1)
    return jnp.pad(w, ((0, 0), (0, PROJ_WIDTH - w.shape[1]))).astype(BF16)


def _pad_lanes(v, width=LANES):
    return jnp.pad(v, (0, width - v.shape[0]))[None, :]


def kernel(x_prompt, x_sample, mem_prompt, state_ssm, state_conv, cache_swa_k, cache_swa_v,
           cache_mem_k, cache_mem_v, norm_mix, w_in, conv_w, conv_b, dt_bias, a_log, d_skip,
           gate_norm, sinks, w_out, norm_mem, norm_x, w_xq, w_xk, w_xv, w_xo, norm_ffn,
           w_gate, w_up, w_down, norm_final):
    bp, tp, _ = x_prompt.shape
    bs, ts, _ = x_sample.shape
    assert ts == DEC_T and tp % CHUNK == 0 and bs % SEQ_TILE == 0

    w_in_p = _permute_w_in(w_in[0])
    w_out_b = w_out[0].astype(BF16)
    w_xq_b = w_xq[0].astype(BF16)
    w_xkv_b = jnp.concatenate([w_xk[0], w_xv[0]], axis=1).astype(BF16)
    w_xo_b = w_xo[0].astype(BF16)
    w_gate_b, w_up_b, w_down_b = (w[0].astype(BF16) for w in (w_gate, w_up, w_down))
    row = lambda v: v.reshape(1, -1)
    cw = jnp.concatenate([conv_w[0][:, :SSM_WIDTH], conv_w[0][:, SSM_WIDTH:]], axis=1)
    cbias = row(conv_b[0])
    dtb = _pad_lanes(dt_bias[0])
    alog = _pad_lanes(a_log[0])
    dskip_e = row(jnp.repeat(d_skip[0], SSM_HEADDIM))
    dtb_e = row(jnp.repeat(dt_bias[0], SSM_HEADDIM))
    alog_e = row(jnp.repeat(a_log[0], SSM_HEADDIM))
    gn = row(gate_norm[0])
    tril = jnp.asarray(np.tril(np.ones((CHUNK, CHUNK), np.float32)), BF16)
    expand = jnp.asarray(np.arange(LANES)[:, None] == (np.arange(SSM_WIDTH)[None, :] // SSM_HEADDIM),
                         BF16)

    xp = x_prompt.reshape(bp * tp, D_MODEL)
    proj_p = norm_matmul(xp, row(norm_mix[0]), w_in_p, tm=512, tn=512)
    y_p, p_ssm = ssd_prompt(proj_p, cw, cbias, dtb, alog, dskip_e, gn, tril, expand,
                            batch=bp, seq=tp)
    cos_p, sa_p, sb_p = _rope_tables(np.arange(tp), 1)
    att_p, p_k = swa_prompt(proj_p, sinks[0], cos_p, sa_p, sb_p, batch=bp, seq=tp)
    x1_p, qx_p = out_proj(y_p, att_p, xp, w_out_b, row(norm_x[0]), w_xq_b, tm=512)
    mkv = norm_matmul(mem_prompt.reshape(bp * N_MEM, D_MODEL), row(norm_mem[0]), w_xkv_b,
                      tm=bp * N_MEM, tn=512)
    x2_p = xattn_prompt(qx_p, mkv, x1_p, w_xo_b, batch=bp, seq=tp, tq=512)
    y_prompt = ffn(x2_p, row(norm_ffn[0]), w_gate_b, w_up_b, w_down_b, row(norm_final),
                   tm=1024, tf=512).reshape(bp, tp, D_MODEL)

    proj_p3 = proj_p.reshape(bp, tp, PROJ_WIDTH)
    tail = proj_p3[:, tp - (CONV_WIDTH - 1):, :]
    p_conv = jnp.concatenate([tail[:, :, COL_X:COL_X + SSM_WIDTH],
                              tail[:, :, COL_BC:COL_BC + BC_WIDTH]], axis=-1)
    p_v = proj_p3[:, tp - WINDOW:, COL_KV + KV_WIDTH:COL_KV + 2 * KV_WIDTH]
    p_mk = mkv[:, :X_WIDTH].reshape(bp, N_MEM, X_HEADS, X_HEAD_DIM)
    p_mv = mkv[:, X_WIDTH:].reshape(bp, N_MEM, X_HEADS, X_HEAD_DIM)

    xs = x_sample.transpose(1, 0, 2).reshape(ts * bs, D_MODEL)
    proj_s = norm_matmul(xs, row(norm_mix[0]), w_in_p, tm=ts * bs, tn=512)
    proj_s3 = proj_s.reshape(ts, bs, PROJ_WIDTH)
    y_s, s_ssm, s_conv = ssd_sample(
        proj_s3, state_conv[0].transpose(1, 0, 2),
        state_ssm[0].reshape(bs, SSM_WIDTH, D_STATE), cw, cbias, dtb_e, alog_e, dskip_e, gn,
        expand, batch=bs)

    rep = ATT_HEADS // KV_HEADS
    q_s = proj_s3[:, :, COL_Q:COL_Q + ATT_WIDTH].reshape(ts, bs, KV_HEADS, rep, HEAD_DIM)
    q_s = q_s.transpose(1, 2, 0, 3, 4)
    zq = jnp.zeros_like(q_s[:, 0])
    q_bd = jnp.stack([jnp.concatenate([q_s[:, 0], zq], axis=-1),
                      jnp.concatenate([zq, q_s[:, 1]], axis=-1)], axis=1)
    q_bd = q_bd.reshape(bs, KV_HEADS * ts * rep, KV_WIDTH)
    kv_s = proj_s3[:, :, COL_KV:COL_KV + 2 * KV_WIDTH].transpose(1, 0, 2)
    kv_s = jnp.pad(kv_s, ((0, 0), (0, SUBLANES - ts), (0, 0)))
    pos_s = PAST_LEN + np.arange(ts)
    ktabs = _rope_tables(PAST_LEN + np.arange(SUBLANES), 1)
    qtabs = tuple(np.tile(t, (KV_HEADS, 1)) for t in _rope_tables(pos_s, rep))
    sink_rows = jnp.broadcast_to(
        jnp.broadcast_to(sinks[0].reshape(KV_HEADS, 1, rep), (KV_HEADS, ts, rep)).reshape(-1, 1),
        (KV_HEADS * ts * rep, LANES))
    o_bd, s_k, s_v = swa_sample(
        q_bd, kv_s[:, :, :KV_WIDTH], kv_s[:, :, KV_WIDTH:],
        cache_swa_k[0].reshape(bs, WINDOW, KV_WIDTH), cache_swa_v[0].reshape(bs, WINDOW, KV_WIDTH),
        sink_rows, qtabs, ktabs, batch=bs)
    o_bd = o_bd.reshape(bs, KV_HEADS, ts, rep, KV_HEADS, HEAD_DIM)
    att_s = jnp.stack([o_bd[:, g, :, :, g, :] for g in range(KV_HEADS)], axis=2)
    att_s = att_s.transpose(1, 0, 2, 3, 4).reshape(ts * bs, ATT_WIDTH).astype(BF16)

    x1_s, qx_s = out_proj(y_s.reshape(ts * bs, SSM_WIDTH), att_s, xs, w_out_b, row(norm_x[0]),
                          w_xq_b, tm=ts * bs)
    qx_s8 = jnp.pad(qx_s.astype(F32).reshape(ts, bs, X_WIDTH).transpose(1, 0, 2),
                    ((0, 0), (0, SUBLANES - ts), (0, 0)))
    o_x = xattn_sample(qx_s8, cache_mem_k.reshape(bs, N_MEM * X_HEADS, X_HEAD_DIM),
                       cache_mem_v.reshape(bs, N_MEM * X_HEADS, X_HEAD_DIM), batch=bs)
    o_x = o_x.reshape(bs, X_HEADS, SUBLANES, X_HEAD_DIM)[:, :, :ts]
    o_x = o_x.transpose(2, 0, 1, 3).reshape(ts * bs, X_WIDTH).astype(BF16)
    x2_s = matmul_residual(o_x, w_xo_b, x1_s, tm=ts * bs)
    y_s_out = ffn(x2_s, row(norm_ffn[0]), w_gate_b, w_up_b, w_down_b, row(norm_final),
                  tm=ts * bs, tf=512)
    y_sample = y_s_out.reshape(ts, bs, D_MODEL).transpose(1, 0, 2)

    return (y_prompt, y_sample,
            p_ssm.reshape(1, bp, SSM_HEADS, SSM_HEADDIM, D_STATE), p_conv[None],
            p_k.reshape(1, bp, WINDOW, KV_HEADS, HEAD_DIM),
            p_v.reshape(1, bp, WINDOW, KV_HEADS, HEAD_DIM),
            p_mk[None], p_mv[None],
            s_ssm.reshape(1, bs, SSM_HEADS, SSM_HEADDIM, D_STATE),
            s_conv.transpose(1, 0, 2)[None],
            s_k.reshape(1, bs, WINDOW, KV_HEADS, HEAD_DIM),
            s_v.reshape(1, bs, WINDOW, KV_HEADS, HEAD_DIM))
```

```python
import functools

import jax
import jax.numpy as jnp
import numpy as np
from jax import lax
from jax.experimental import pallas as pl
from jax.experimental.pallas import tpu as pltpu

F32 = jnp.float32
BF16 = jnp.bfloat16

D_MODEL = 2048
SSM_WIDTH = 1024
SSM_HEADDIM = 64
SSM_HEADS = 16
SSM_GROUPS = 2
D_STATE = 128
CONV_WIDTH = 4
CONV_DIM = SSM_WIDTH + 2 * SSM_GROUPS * D_STATE
BC_WIDTH = 2 * SSM_GROUPS * D_STATE
ATT_WIDTH = 1024
HEAD_DIM = 64
ATT_HEADS = 16
KV_HEADS = 2
KV_WIDTH = KV_HEADS * HEAD_DIM
WINDOW = 128
CHUNK = 128
ROT_DIM = 16
ROPE_THETA = 500000.0
PAST_LEN = 16384
N_MEM = 256
X_HEADS = 4
X_HEAD_DIM = 128
X_WIDTH = X_HEADS * X_HEAD_DIM
EPS = 1e-5

LANES = 128
SUBLANES = 8
VMEM_LIMIT_BYTES = 56 * 1024 * 1024

PROJ_WIDTH = 3968
COL_Z = 0
COL_X = 1024
COL_BC = 2048
COL_Q = 2560
COL_KV = 3584
COL_DT = 3840
Q_BLOCK = 512

NEG_INF = float("-inf")


def _params(*semantics):
    return pltpu.CompilerParams(dimension_semantics=semantics,
                                vmem_limit_bytes=VMEM_LIMIT_BYTES)


def _rmsnorm(x, gain):
    ms = jnp.mean(x * x, axis=-1, keepdims=True)
    return x * lax.rsqrt(ms + EPS) * gain


def _silu(x):
    return x * jax.nn.sigmoid(x)


def _softplus(x):
    return jnp.maximum(x, 0.0) + jnp.log1p(jnp.exp(-jnp.abs(x)))


def _split_bf16(v, parts):
    out = []
    rem = v
    for _ in range(parts - 1):
        p = rem.astype(BF16)
        out.append(p)
        rem = rem - p.astype(F32)
    out.append(rem.astype(BF16))
    return out


def _dot(a, b):
    return jnp.dot(a, b, preferred_element_type=F32)


def _dot_nt(a, b):
    return lax.dot_general(a, b, (((1,), (1,)), ((), ())), preferred_element_type=F32)


def _dot_tn(a, b):
    return lax.dot_general(a, b, (((0,), (0,)), ((), ())), preferred_element_type=F32)


def _select_matmul(v, sel, parts):
    acc = None
    for p in _split_bf16(v, parts):
        t = _dot(p, sel)
        acc = t if acc is None else acc + t
    return acc


def _norm_matmul_kernel(x_ref, g_ref, *refs, tn):
    w_refs, o_ref = refs[:-1], refs[-1]
    h = _rmsnorm(x_ref[...], g_ref[...]).astype(BF16)
    col = 0
    for w_ref in w_refs:
        width = w_ref.shape[1]
        for j in range(0, width, tn):
            n = min(tn, width - j)
            o_ref[:, col + j:col + j + n] = _dot(h, w_ref[:, j:j + n]).astype(o_ref.dtype)
        col += width


def norm_matmul(x, gain, ws, *, tm, tn, out_dtype=F32):
    m, k = x.shape
    n = sum(w.shape[1] for w in ws)
    return pl.pallas_call(
        functools.partial(_norm_matmul_kernel, tn=tn),
        grid=(m // tm,),
        in_specs=[pl.BlockSpec((tm, k), lambda i: (i, 0)),
                  pl.BlockSpec((1, k), lambda i: (0, 0))]
                 + [pl.BlockSpec(w.shape, lambda i: (0, 0), pipeline_mode=pl.Buffered(1))
                    for w in ws],
        out_specs=pl.BlockSpec((tm, n), lambda i: (i, 0)),
        out_shape=jax.ShapeDtypeStruct((m, n), out_dtype),
        compiler_params=_params("parallel"),
        name="norm_matmul",
    )(x, gain, *ws)


def _ssd_prompt_kernel(z_ref, x_ref, bc_ref, dt_ref, cw_ref, cb_ref, dtb_ref, alog_ref,
                       dskip_ref, gn_ref, tril_ref, exp_ref,
                       y_ref, state_ref, carry_ref, st_ref):
    c = pl.program_id(1)
    nc = pl.num_programs(1)
    t = CHUNK

    @pl.when(c == 0)
    def _():
        carry_ref[...] = jnp.zeros_like(carry_ref)
        st_ref[...] = jnp.zeros_like(st_ref)

    x_new = jnp.concatenate([x_ref[...], bc_ref[...]], axis=1)
    cat = jnp.concatenate([carry_ref[...], x_new], axis=0)
    acc = jnp.broadcast_to(cb_ref[...], (t, CONV_DIM))
    for k in range(CONV_WIDTH):
        back = CONV_WIDTH - 1 - k
        tap = x_new if back == 0 else pltpu.roll(cat, back, axis=0)[SUBLANES:SUBLANES + t]
        acc = acc + tap * cw_ref[k:k + 1, :]
    xbc = _silu(acc)
    carry_ref[...] = x_new[t - SUBLANES:t]

    xs = xbc[:, 0:SSM_WIDTH]
    xs_b = xs.astype(BF16)
    b_mat = [xbc[:, SSM_WIDTH + g * D_STATE:SSM_WIDTH + (g + 1) * D_STATE].astype(BF16)
             for g in range(SSM_GROUPS)]
    c_off = SSM_WIDTH + SSM_GROUPS * D_STATE
    c_mat = [xbc[:, c_off + g * D_STATE:c_off + (g + 1) * D_STATE].astype(BF16)
             for g in range(SSM_GROUPS)]

    dt = _softplus(dt_ref[...] + dtb_ref[...])
    a = -jnp.exp(alog_ref[...])
    da = dt * a
    cs = _select_matmul_left(tril_ref[...], da)
    cs_t = cs.T
    dt_t = dt.T
    cs_last = cs[t - 1:t, :]
    exp_cs = jnp.exp(cs)
    to_end = jnp.exp(cs_last - cs) * dt
    sel = exp_ref[...]
    exp_cs_e = _select_matmul(exp_cs, sel, 2)
    to_end_e = _select_matmul(to_end, sel, 2)

    row = lax.broadcasted_iota(jnp.int32, (t, 2 * t), 0)
    col = lax.broadcasted_iota(jnp.int32, (t, 2 * t), 1)
    causal = jnp.bitwise_and(col, t - 1) <= row
    lane = lax.broadcasted_iota(jnp.int32, (t, LANES), 1)
    left = lane < SSM_HEADDIM

    cb = [_dot_nt(c_mat[g], b_mat[g]) for g in range(SSM_GROUPS)]
    cb2 = [jnp.concatenate([m, m], axis=1) for m in cb]

    y_parts = []
    for j in range(SSM_HEADS // 2):
        g = (2 * j) // (SSM_HEADS // SSM_GROUPS)
        h0, h1 = 2 * j, 2 * j + 1
        colv = jnp.concatenate([jnp.broadcast_to(cs[:, h0:h0 + 1], (t, t)),
                                jnp.broadcast_to(cs[:, h1:h1 + 1], (t, t))], axis=1)
        rowv = jnp.concatenate([jnp.broadcast_to(cs_t[h0:h0 + 1, :], (t, t)),
                                jnp.broadcast_to(cs_t[h1:h1 + 1, :], (t, t))], axis=1)
        dtr = jnp.concatenate([jnp.broadcast_to(dt_t[h0:h0 + 1, :], (t, t)),
                               jnp.broadcast_to(dt_t[h1:h1 + 1, :], (t, t))], axis=1)
        decay = jnp.where(causal, jnp.exp(colv - rowv), 0.0)
        wts = (cb2[g] * decay * dtr).astype(BF16)
        xp = xs_b[:, j * LANES:(j + 1) * LANES]
        zero = jnp.zeros_like(xp)
        rhs = jnp.concatenate([jnp.where(left, xp, zero), jnp.where(left, zero, xp)], axis=0)
        y_parts.append(_dot(wts, rhs))
    y_diag = jnp.concatenate(y_parts, axis=1)

    half = SSM_WIDTH // SSM_GROUPS
    st = st_ref[...]
    st_b = st.astype(BF16)
    y_off = jnp.concatenate(
        [_dot(c_mat[g], st_b[:, g * half:(g + 1) * half]) for g in range(SSM_GROUPS)], axis=1)
    xw = (xs * to_end_e).astype(BF16)
    upd = jnp.concatenate(
        [_dot_tn(b_mat[g], xw[:, g * half:(g + 1) * half]) for g in range(SSM_GROUPS)], axis=1)
    st_new = st * exp_cs_e[t - 1:t, :] + upd
    st_ref[...] = st_new

    y = y_diag + y_off * exp_cs_e + xs * dskip_ref[...]
    gated = y * _silu(z_ref[...])
    outs = []
    for g in range(SSM_GROUPS):
        outs.append(_rmsnorm(gated[:, g * half:(g + 1) * half], gn_ref[:, g * half:(g + 1) * half]))
    y_ref[...] = jnp.concatenate(outs, axis=1).astype(y_ref.dtype)

    @pl.when(c == nc - 1)
    def _():
        state_ref[0] = st_new.T


def _select_matmul_left(sel, v):
    acc = None
    for p in _split_bf16(v, 3):
        t = _dot(sel, p)
        acc = t if acc is None else acc + t
    return acc


def ssd_prompt(proj, conv_w, conv_b, dtb, alog, dskip_e, gate_norm, tril, expand, *, batch, seq):
    nc = seq // CHUNK
    rows = lambda b, c: b * nc + c
    const = lambda b, c: (0, 0)
    return pl.pallas_call(
        _ssd_prompt_kernel,
        grid=(batch, nc),
        in_specs=[
            pl.BlockSpec((CHUNK, SSM_WIDTH), lambda b, c: (rows(b, c), COL_Z // SSM_WIDTH)),
            pl.BlockSpec((CHUNK, SSM_WIDTH), lambda b, c: (rows(b, c), COL_X // SSM_WIDTH)),
            pl.BlockSpec((CHUNK, BC_WIDTH), lambda b, c: (rows(b, c), COL_BC // BC_WIDTH)),
            pl.BlockSpec((CHUNK, LANES), lambda b, c: (rows(b, c), COL_DT // LANES)),
            pl.BlockSpec((CONV_WIDTH, CONV_DIM), const),
            pl.BlockSpec((1, CONV_DIM), const),
            pl.BlockSpec((1, LANES), const),
            pl.BlockSpec((1, LANES), const),
            pl.BlockSpec((1, SSM_WIDTH), const),
            pl.BlockSpec((1, SSM_WIDTH), const),
            pl.BlockSpec((CHUNK, CHUNK), const),
            pl.BlockSpec((LANES, SSM_WIDTH), const),
        ],
        out_specs=[pl.BlockSpec((CHUNK, SSM_WIDTH), lambda b, c: (rows(b, c), 0)),
                   pl.BlockSpec((1, SSM_WIDTH, D_STATE), lambda b, c: (b, 0, 0))],
        out_shape=[jax.ShapeDtypeStruct((batch * seq, SSM_WIDTH), BF16),
                   jax.ShapeDtypeStruct((batch, SSM_WIDTH, D_STATE), F32)],
        scratch_shapes=[pltpu.VMEM((SUBLANES, CONV_DIM), F32),
                        pltpu.VMEM((D_STATE, SSM_WIDTH), F32)],
        compiler_params=_params("parallel", "arbitrary"),
        name="ssd_prompt",
    )(proj, proj, proj, proj, conv_w, conv_b, dtb, alog, dskip_e, gate_norm, tril, expand)


def _rope(x, cos, sin_a, sin_b):
    parts = []
    for i in range(x.shape[1] // LANES):
        xb = x[:, i * LANES:(i + 1) * LANES]
        up = pltpu.roll(xb, LANES - ROT_DIM // 2, axis=1)
        dn = pltpu.roll(xb, ROT_DIM // 2, axis=1)
        parts.append(xb * cos + up * sin_a + dn * sin_b)
    return parts[0] if len(parts) == 1 else jnp.concatenate(parts, axis=1)


def _block_diag_rows(a, a_swapped, first):
    lane = lax.broadcasted_iota(jnp.int32, a.shape, 1)
    left = lane < HEAD_DIM
    zero = jnp.zeros_like(a)
    if first == 0:
        top, bot = jnp.where(left, a, zero), jnp.where(left, zero, a_swapped)
    else:
        top, bot = jnp.where(left, a_swapped, zero), jnp.where(left, zero, a)
    return jnp.concatenate([top, bot], axis=0)


def _swa_prompt_kernel(sink_ref, qlo_ref, qhi_ref, kv_ref, cos_ref, sa_ref, sb_ref,
                       att_ref, k_ref, kprev_ref, vprev_ref, s_ref, p_ref):
    blk = pl.program_id(1)
    w = WINDOW
    pairs = ATT_HEADS // 2

    @pl.when(blk == 0)
    def _():
        kprev_ref[...] = jnp.zeros_like(kprev_ref)
        vprev_ref[...] = jnp.zeros_like(vprev_ref)

    cos, sa, sb = cos_ref[...], sa_ref[...], sb_ref[...]
    k_cur = _rope(kv_ref[:, 0:KV_WIDTH], cos, sa, sb)
    v_cur = kv_ref[:, KV_WIDTH:2 * KV_WIDTH]
    k_ref[0] = k_cur
    q = jnp.concatenate([qlo_ref[...], qhi_ref[...]], axis=1)
    q = (_rope(q, cos, sa, sb) * (HEAD_DIM ** -0.5)).astype(BF16)

    k_cat = jnp.concatenate([kprev_ref[...], k_cur], axis=0)
    v_cat = jnp.concatenate([vprev_ref[...], v_cur], axis=0)
    k_sw = pltpu.roll(k_cat, HEAD_DIM, axis=1)
    v_sw = pltpu.roll(v_cat, HEAD_DIM, axis=1)
    kbd = [_block_diag_rows(k_cat, k_sw, g).astype(BF16) for g in range(KV_HEADS)]
    vbd_t = [_block_diag_rows(v_cat, v_sw, g).T.astype(BF16) for g in range(KV_HEADS)]

    for j in range(pairs):
        g = (2 * j) // (ATT_HEADS // KV_HEADS)
        s_ref[j] = _dot_nt(kbd[g], q[:, j * LANES:(j + 1) * LANES])

    key = lax.broadcasted_iota(jnp.int32, (w, w), 0)
    qry = lax.broadcasted_iota(jnp.int32, (w, w), 1)
    own_valid = key <= qry
    has_prev = blk > 0
    for j in range(pairs):
        for hh in range(2):
            base = hh * 2 * w
            sink = sink_ref[2 * j + hh]
            prev = jnp.where(has_prev, s_ref[j, base:base + w, :], NEG_INF)
            s = jnp.where(own_valid, s_ref[j, base + w:base + 2 * w, :], prev)
            m = jnp.maximum(jnp.max(s, axis=0, keepdims=True), sink)
            e = jnp.exp(s - m)
            den = jnp.sum(e, axis=0, keepdims=True) + jnp.exp(sink - m)
            p = e * (1.0 / den)
            p_ref[j, base:base + w, :] = jnp.where(own_valid, 0.0, p).astype(BF16)
            p_ref[j, base + w:base + 2 * w, :] = jnp.where(own_valid, p, 0.0).astype(BF16)

    for j in range(pairs):
        g = (2 * j) // (ATT_HEADS // KV_HEADS)
        o_t = _dot(vbd_t[g], p_ref[j])
        att_ref[:, j * LANES:(j + 1) * LANES] = o_t.T.astype(att_ref.dtype)

    kprev_ref[...] = k_cur
    vprev_ref[...] = v_cur


def swa_prompt(proj, sinks, cos, sin_a, sin_b, *, batch, seq):
    nb = seq // WINDOW
    rows = lambda b, i: b * nb + i
    return pl.pallas_call(
        _swa_prompt_kernel,
        grid=(batch, nb),
        in_specs=[
            pl.BlockSpec(memory_space=pltpu.SMEM),
            pl.BlockSpec((WINDOW, Q_BLOCK), lambda b, i: (rows(b, i), COL_Q // Q_BLOCK)),
            pl.BlockSpec((WINDOW, Q_BLOCK), lambda b, i: (rows(b, i), COL_Q // Q_BLOCK + 1)),
            pl.BlockSpec((WINDOW, 2 * KV_WIDTH), lambda b, i: (rows(b, i), COL_KV // (2 * KV_WIDTH))),
            pl.BlockSpec((WINDOW, LANES), lambda b, i: (i, 0)),
            pl.BlockSpec((WINDOW, LANES), lambda b, i: (i, 0)),
            pl.BlockSpec((WINDOW, LANES), lambda b, i: (i, 0)),
        ],
        out_specs=[pl.BlockSpec((WINDOW, ATT_WIDTH), lambda b, i: (rows(b, i), 0)),
                   pl.BlockSpec((1, WINDOW, KV_WIDTH), lambda b, i: (b, 0, 0))],
        out_shape=[jax.ShapeDtypeStruct((batch * seq, ATT_WIDTH), BF16),
                   jax.ShapeDtypeStruct((batch, WINDOW, KV_WIDTH), F32)],
        scratch_shapes=[pltpu.VMEM((WINDOW, KV_WIDTH), F32),
                        pltpu.VMEM((WINDOW, KV_WIDTH), F32),
                        pltpu.VMEM((ATT_HEADS // 2, 4 * WINDOW, WINDOW), F32),
                        pltpu.VMEM((ATT_HEADS // 2, 4 * WINDOW, WINDOW), BF16)],
        compiler_params=_params("parallel", "arbitrary"),
        name="swa_prompt",
    )(sinks, proj, proj, proj, cos, sin_a, sin_b)


def _out_proj_kernel(y_ref, a_ref, x_ref, wo_ref, gx_ref, wq_ref, x1_ref, q_ref):
    mix = _dot(y_ref[...], wo_ref[0:SSM_WIDTH, :]) + _dot(a_ref[...], wo_ref[SSM_WIDTH:, :])
    x1 = x_ref[...] + mix
    x1_ref[...] = x1
    h = _rmsnorm(x1, gx_ref[...]).astype(BF16)
    q_ref[...] = _dot(h, wq_ref[...]).astype(q_ref.dtype)


def out_proj(y, att, x, w_out, norm_x, w_xq, *, tm):
    m = x.shape[0]
    const = lambda i: (0, 0)
    return pl.pallas_call(
        _out_proj_kernel,
        grid=(m // tm,),
        in_specs=[pl.BlockSpec((tm, SSM_WIDTH), lambda i: (i, 0)),
                  pl.BlockSpec((tm, ATT_WIDTH), lambda i: (i, 0)),
                  pl.BlockSpec((tm, D_MODEL), lambda i: (i, 0)),
                  pl.BlockSpec((D_MODEL, D_MODEL), const),
                  pl.BlockSpec((1, D_MODEL), const),
                  pl.BlockSpec((D_MODEL, X_WIDTH), const)],
        out_specs=[pl.BlockSpec((tm, D_MODEL), lambda i: (i, 0)),
                   pl.BlockSpec((tm, X_WIDTH), lambda i: (i, 0))],
        out_shape=[jax.ShapeDtypeStruct((m, D_MODEL), F32),
                   jax.ShapeDtypeStruct((m, X_WIDTH), BF16)],
        compiler_params=_params("parallel"),
        name="out_proj",
    )(y, att, x, w_out, norm_x, w_xq)


def _softmax(s):
    m = jnp.max(s, axis=-1, keepdims=True)
    e = jnp.exp(s - m)
    return e / jnp.sum(e, axis=-1, keepdims=True)


def _xattn_prompt_kernel(q_ref, mk_ref, mv_ref, x1_ref, wo_ref, x2_ref):
    mk = mk_ref[...].astype(BF16)
    mv = mv_ref[...].astype(BF16)
    outs = []
    for h in range(X_HEADS):
        sl = slice(h * X_HEAD_DIM, (h + 1) * X_HEAD_DIM)
        s = _dot_nt(q_ref[:, sl], mk[:, sl]) * (X_HEAD_DIM ** -0.5)
        outs.append(_dot(_softmax(s).astype(BF16), mv[:, sl]).astype(BF16))
    o = jnp.concatenate(outs, axis=1)
    x2_ref[...] = x1_ref[...] + _dot(o, wo_ref[...])


def xattn_prompt(q, mkv, x1, w_xo, *, batch, seq, tq):
    nq = seq // tq
    return pl.pallas_call(
        _xattn_prompt_kernel,
        grid=(batch, nq),
        in_specs=[pl.BlockSpec((tq, X_WIDTH), lambda b, i: (b * nq + i, 0)),
                  pl.BlockSpec((N_MEM, X_WIDTH), lambda b, i: (b, 0)),
                  pl.BlockSpec((N_MEM, X_WIDTH), lambda b, i: (b, 1)),
                  pl.BlockSpec((tq, D_MODEL), lambda b, i: (b * nq + i, 0)),
                  pl.BlockSpec((X_WIDTH, D_MODEL), lambda b, i: (0, 0))],
        out_specs=pl.BlockSpec((tq, D_MODEL), lambda b, i: (b * nq + i, 0)),
        out_shape=jax.ShapeDtypeStruct(x1.shape, F32),
        compiler_params=_params("parallel", "parallel"),
        name="xattn_prompt",
    )(q, mkv, mkv, x1, w_xo)


def _ffn_kernel(x_ref, gn_ref, wg_ref, wu_ref, wd_ref, gf_ref, o_ref, h_ref):
    f = pl.program_id(1)

    @pl.when(f == 0)
    def _():
        x = x_ref[...]
        h_ref[...] = _rmsnorm(x, gn_ref[...]).astype(BF16)
        o_ref[...] = x

    h = h_ref[...]
    act = (_silu(_dot(h, wg_ref[...])) * _dot(h, wu_ref[...])).astype(BF16)
    o_ref[...] += _dot(act, wd_ref[...])

    @pl.when(f == pl.num_programs(1) - 1)
    def _():
        o_ref[...] = _rmsnorm(o_ref[...], gf_ref[...])


def ffn(x, norm_ffn, w_gate, w_up, w_down, norm_final, *, tm, tf):
    m = x.shape[0]
    d_ff = w_gate.shape[1]
    return pl.pallas_call(
        _ffn_kernel,
        grid=(m // tm, d_ff // tf),
        in_specs=[pl.BlockSpec((tm, D_MODEL), lambda i, f: (i, 0)),
                  pl.BlockSpec((1, D_MODEL), lambda i, f: (0, 0)),
                  pl.BlockSpec((D_MODEL, tf), lambda i, f: (0, f)),
                  pl.BlockSpec((D_MODEL, tf), lambda i, f: (0, f)),
                  pl.BlockSpec((tf, D_MODEL), lambda i, f: (f, 0)),
                  pl.BlockSpec((1, D_MODEL), lambda i, f: (0, 0))],
        out_specs=pl.BlockSpec((tm, D_MODEL), lambda i, f: (i, 0)),
        out_shape=jax.ShapeDtypeStruct((m, D_MODEL), F32),
        scratch_shapes=[pltpu.VMEM((tm, D_MODEL), BF16)],
        compiler_params=_params("parallel", "arbitrary"),
        name="ffn",
    )(x, norm_ffn, w_gate, w_up, w_down, norm_final)


SEQ_TILE = SUBLANES
DEC_T = 4


def _ssd_sample_kernel(z_ref, x_ref, bc_ref, dt_ref, conv0_ref, h0_ref, cw_ref, cb_ref,
                       dtbe_ref, aloge_ref, dskip_ref, gn_ref, exp_ref,
                       y_ref, hnew_ref, conv_ref):
    nt, sb = DEC_T, SEQ_TILE
    half = SSM_WIDTH // SSM_GROUPS

    xin = [conv0_ref[j] for j in range(CONV_WIDTH - 1)]
    xin += [jnp.concatenate([x_ref[u], bc_ref[u]], axis=1) for u in range(nt)]
    for j in range(CONV_WIDTH - 1):
        conv_ref[j] = xin[nt + j]
    xbc = []
    for u in range(nt):
        acc = jnp.broadcast_to(cb_ref[...], (sb, CONV_DIM))
        for k in range(CONV_WIDTH):
            acc = acc + xin[u + k] * cw_ref[k:k + 1, :]
        xbc.append(_silu(acc))
    xs = [v[:, 0:SSM_WIDTH] for v in xbc]
    c_off = SSM_WIDTH + SSM_GROUPS * D_STATE
    b_rows = [v[:, SSM_WIDTH:c_off] for v in xbc]
    c_rows = [v[:, c_off:CONV_DIM] for v in xbc]

    dt_raw = jnp.concatenate([dt_ref[u] for u in range(nt)], axis=0)
    dt_e = _softplus(_select_matmul(dt_raw, exp_ref[...], 3) + dtbe_ref[...])
    da_e = dt_e * -jnp.exp(aloge_ref[...])
    dts = [dt_e[u * sb:(u + 1) * sb] for u in range(nt)]
    cs = []
    for u in range(nt):
        d = da_e[u * sb:(u + 1) * sb]
        cs.append(d if u == 0 else cs[-1] + d)

    lane = lax.broadcasted_iota(jnp.int32, (sb, SSM_WIDTH), 1)
    first_group = lane < half

    def group_bcast(v0, v1):
        return jnp.where(first_group, v0, v1)

    y = []
    for u in range(nt):
        acc = None
        for s in range(u + 1):
            prod = c_rows[u] * b_rows[s]
            cbv = [jnp.sum(prod[:, g * D_STATE:(g + 1) * D_STATE], axis=-1, keepdims=True)
                   for g in range(SSM_GROUPS)]
            coef = group_bcast(cbv[0], cbv[1]) * jnp.exp(cs[u] - cs[s]) * dts[s]
            term = coef * xs[s]
            acc = term if acc is None else acc + term
        y.append(acc)

    c_stack = jnp.concatenate(c_rows, axis=0).astype(BF16)
    to_end = [jnp.exp(cs[nt - 1] - cs[u]) * dts[u] for u in range(nt)]
    xw = [xs[u] * to_end[u] for u in range(nt)]
    dec_parts = [p.astype(F32) for p in _split_bf16(jnp.exp(cs[nt - 1]), 3)]
    pad_rows = jnp.zeros((sb, SSM_WIDTH), F32)
    lhs_t = jnp.concatenate(xw + dec_parts + [pad_rows], axis=0).T.astype(BF16)
    ones = jnp.ones((sb, D_STATE), F32)
    zeros = jnp.zeros((sb, D_STATE), F32)
    rhs = []
    for g in range(SSM_GROUPS):
        bg = [v[:, g * D_STATE:(g + 1) * D_STATE] for v in b_rows]
        left = jnp.concatenate(bg + [zeros] * 4, axis=0)
        right = jnp.concatenate([zeros] * nt + [ones] * 3 + [zeros], axis=0)
        rhs.append(jnp.concatenate([left, right], axis=1))
    krow = jnp.bitwise_and(lax.broadcasted_iota(jnp.int32, (2 * nt * sb, 2 * D_STATE), 0), sb - 1)
    yrow = jnp.bitwise_and(lax.broadcasted_iota(jnp.int32, (nt * sb, SSM_WIDTH), 0), sb - 1)

    y_off = jnp.zeros((nt * sb, SSM_WIDTH), F32)
    for i in range(sb):
        h0 = h0_ref[i]
        h0_b = h0.astype(BF16)
        off = jnp.concatenate(
            [_dot_nt(c_stack[:, g * D_STATE:(g + 1) * D_STATE], h0_b[g * half:(g + 1) * half])
             for g in range(SSM_GROUPS)], axis=1)
        y_off = y_off + jnp.where(yrow == i, off, 0.0)
        new = []
        for g in range(SSM_GROUPS):
            r = jnp.where(krow == i, rhs[g], 0.0).astype(BF16)
            res = _dot(lhs_t[g * half:(g + 1) * half], r)
            new.append(h0[g * half:(g + 1) * half] * res[:, D_STATE:] + res[:, :D_STATE])
        hnew_ref[i] = jnp.concatenate(new, axis=0)

    for u in range(nt):
        yu = y[u] + y_off[u * sb:(u + 1) * sb] * jnp.exp(cs[u]) + xs[u] * dskip_ref[...]
        gated = yu * _silu(z_ref[u])
        outs = [_rmsnorm(gated[:, g * half:(g + 1) * half], gn_ref[:, g * half:(g + 1) * half])
                for g in range(SSM_GROUPS)]
        y_ref[u] = jnp.concatenate(outs, axis=1).astype(y_ref.dtype)


def ssd_sample(proj, conv0, h0, conv_w, conv_b, dtb_e, alog_e, dskip_e, gate_norm, expand, *,
               batch):
    sb = SEQ_TILE
    const = lambda i: (0, 0)
    return pl.pallas_call(
        _ssd_sample_kernel,
        grid=(batch // sb,),
        in_specs=[
            pl.BlockSpec((DEC_T, sb, SSM_WIDTH), lambda i: (0, i, COL_Z // SSM_WIDTH)),
            pl.BlockSpec((DEC_T, sb, SSM_WIDTH), lambda i: (0, i, COL_X // SSM_WIDTH)),
            pl.BlockSpec((DEC_T, sb, BC_WIDTH), lambda i: (0, i, COL_BC // BC_WIDTH)),
            pl.BlockSpec((DEC_T, sb, LANES), lambda i: (0, i, COL_DT // LANES)),
            pl.BlockSpec((CONV_WIDTH - 1, sb, CONV_DIM), lambda i: (0, i, 0)),
            pl.BlockSpec((sb, SSM_WIDTH, D_STATE), lambda i: (i, 0, 0)),
            pl.BlockSpec((CONV_WIDTH, CONV_DIM), const),
            pl.BlockSpec((1, CONV_DIM), const),
            pl.BlockSpec((1, SSM_WIDTH), const),
            pl.BlockSpec((1, SSM_WIDTH), const),
            pl.BlockSpec((1, SSM_WIDTH), const),
            pl.BlockSpec((1, SSM_WIDTH), const),
            pl.BlockSpec((LANES, SSM_WIDTH), const),
        ],
        out_specs=[pl.BlockSpec((DEC_T, sb, SSM_WIDTH), lambda i: (0, i, 0)),
                   pl.BlockSpec((sb, SSM_WIDTH, D_STATE), lambda i: (i, 0, 0)),
                   pl.BlockSpec((CONV_WIDTH - 1, sb, CONV_DIM), lambda i: (0, i, 0))],
        out_shape=[jax.ShapeDtypeStruct((DEC_T, batch, SSM_WIDTH), BF16),
                   jax.ShapeDtypeStruct((batch, SSM_WIDTH, D_STATE), F32),
                   jax.ShapeDtypeStruct((CONV_WIDTH - 1, batch, CONV_DIM), F32)],
        compiler_params=_params("parallel"),
        name="ssd_sample",
    )(proj, proj, proj, proj, conv0, h0, conv_w, conv_b, dtb_e, alog_e, dskip_e, gate_norm, expand)


def _swa_sample_kernel(q_ref, kn_ref, vn_ref, kc_ref, vc_ref, sink_ref, qcos_ref, qsa_ref,
                       qsb_ref, kcos_ref, ksa_ref, ksb_ref, o_ref, ko_ref, vo_ref):
    w = WINDOW
    nt = DEC_T
    nq = KV_HEADS * nt * (ATT_HEADS // KV_HEADS)
    def token_of(rows):
        return jnp.bitwise_and(jnp.right_shift(rows, 3), nt - 1)

    row_t = token_of(lax.broadcasted_iota(jnp.int32, (nq, w), 0))
    col = lax.broadcasted_iota(jnp.int32, (nq, w), 1)
    mask_c = col > row_t
    col_n = lax.broadcasted_iota(jnp.int32, (nq, SUBLANES), 1)
    row_n = token_of(lax.broadcasted_iota(jnp.int32, (nq, SUBLANES), 0))
    mask_n = col_n <= row_n
    sink = sink_ref[...][:, 0:1]

    def body(i, carry):
        q = (_rope(q_ref[i], qcos_ref[...], qsa_ref[...], qsb_ref[...])
             * (HEAD_DIM ** -0.5)).astype(BF16)
        kn8 = _rope(kn_ref[i], kcos_ref[...], ksa_ref[...], ksb_ref[...])
        vn8 = vn_ref[i]
        kn = kn8[0:nt]
        vn = vn8[0:nt]
        kc = kc_ref[i]
        vc = vc_ref[i]
        s_c = jnp.where(mask_c, _dot_nt(q, kc.astype(BF16)), NEG_INF)
        s_n = jnp.where(mask_n, _dot_nt(q, kn8.astype(BF16)), NEG_INF)
        m = jnp.maximum(jnp.maximum(jnp.max(s_c, axis=-1, keepdims=True),
                                    jnp.max(s_n, axis=-1, keepdims=True)), sink)
        e_c = jnp.exp(s_c - m)
        e_n = jnp.exp(s_n - m)
        den = (jnp.sum(e_c, axis=-1, keepdims=True) + jnp.sum(e_n, axis=-1, keepdims=True)
               + jnp.exp(sink - m))
        o = _dot((e_c / den).astype(BF16), vc.astype(BF16)) + _dot((e_n / den).astype(BF16),
                                                                    vn8.astype(BF16))
        o_ref[i] = o.astype(o_ref.dtype)
        ko_ref[i, 0:w - nt, :] = kc_ref[i, nt:w, :]
        ko_ref[i, w - nt:w, :] = kn
        vo_ref[i, 0:w - nt, :] = vc_ref[i, nt:w, :]
        vo_ref[i, w - nt:w, :] = vn
        return carry

    lax.fori_loop(0, q_ref.shape[0], body, 0, unroll=True)


def swa_sample(q, k_new, v_new, k_cache, v_cache, sink_rows, qtabs, ktabs, *, batch):
    sb = SEQ_TILE
    nq = q.shape[1]
    const = lambda i: (0, 0)
    seq3 = lambda i: (i, 0, 0)
    return pl.pallas_call(
        _swa_sample_kernel,
        grid=(batch // sb,),
        in_specs=[pl.BlockSpec((sb, nq, KV_WIDTH), seq3),
                  pl.BlockSpec((sb, SUBLANES, KV_WIDTH), seq3),
                  pl.BlockSpec((sb, SUBLANES, KV_WIDTH), seq3),
                  pl.BlockSpec((sb, WINDOW, KV_WIDTH), seq3),
                  pl.BlockSpec((sb, WINDOW, KV_WIDTH), seq3),
                  pl.BlockSpec((nq, LANES), const),
                  pl.BlockSpec((nq, LANES), const),
                  pl.BlockSpec((nq, LANES), const),
                  pl.BlockSpec((nq, LANES), const),
                  pl.BlockSpec((SUBLANES, LANES), const),
                  pl.BlockSpec((SUBLANES, LANES), const),
                  pl.BlockSpec((SUBLANES, LANES), const)],
        out_specs=[pl.BlockSpec((sb, nq, KV_WIDTH), seq3),
                   pl.BlockSpec((sb, WINDOW, KV_WIDTH), seq3),
                   pl.BlockSpec((sb, WINDOW, KV_WIDTH), seq3)],
        out_shape=[jax.ShapeDtypeStruct((batch, nq, KV_WIDTH), F32),
                   jax.ShapeDtypeStruct((batch, WINDOW, KV_WIDTH), F32),
                   jax.ShapeDtypeStruct((batch, WINDOW, KV_WIDTH), F32)],
        compiler_params=_params("parallel"),
        name="swa_sample",
    )(q, k_new, v_new, k_cache, v_cache, sink_rows, *qtabs, *ktabs)


def _xattn_sample_kernel(q_ref, mk_ref, mv_ref, o_ref):
    rows = X_HEADS * SUBLANES
    row_head = jnp.right_shift(lax.broadcasted_iota(jnp.int32, (rows, N_MEM * X_HEADS), 0), 3)
    col_head = jnp.bitwise_and(lax.broadcasted_iota(jnp.int32, (rows, N_MEM * X_HEADS), 1),
                               X_HEADS - 1)
    own_head = row_head == col_head

    def body(i, carry):
        q8 = q_ref[i]
        qh = jnp.concatenate([q8[:, h * X_HEAD_DIM:(h + 1) * X_HEAD_DIM] for h in range(X_HEADS)],
                             axis=0).astype(BF16)
        s = _dot_nt(qh, mk_ref[i].astype(BF16)) * (X_HEAD_DIM ** -0.5)
        p = _softmax(jnp.where(own_head, s, NEG_INF))
        o_ref[i] = _dot(p.astype(BF16), mv_ref[i].astype(BF16))
        return carry

    lax.fori_loop(0, q_ref.shape[0], body, 0, unroll=True)


def xattn_sample(q, mk, mv, *, batch):
    sb = SEQ_TILE
    seq3 = lambda i: (i, 0, 0)
    rows = X_HEADS * SUBLANES
    return pl.pallas_call(
        _xattn_sample_kernel,
        grid=(batch // sb,),
        in_specs=[pl.BlockSpec((sb, SUBLANES, X_WIDTH), seq3),
                  pl.BlockSpec((sb, N_MEM * X_HEADS, X_HEAD_DIM), seq3),
                  pl.BlockSpec((sb, N_MEM * X_HEADS, X_HEAD_DIM), seq3)],
        out_specs=pl.BlockSpec((sb, rows, X_HEAD_DIM), seq3),
        out_shape=jax.ShapeDtypeStruct((batch, rows, X_HEAD_DIM), F32),
        compiler_params=_params("parallel"),
        name="xattn_sample",
    )(q, mk, mv)


def _matmul_residual_kernel(a_ref, w_ref, x_ref, o_ref):
    o_ref[...] = x_ref[...] + _dot(a_ref[...], w_ref[...])


def matmul_residual(a, w, x, *, tm):
    m, k = a.shape
    n = w.shape[1]
    return pl.pallas_call(
        _matmul_residual_kernel,
        grid=(m // tm,),
        in_specs=[pl.BlockSpec((tm, k), lambda i: (i, 0)),
                  pl.BlockSpec((k, n), lambda i: (0, 0)),
                  pl.BlockSpec((tm, n), lambda i: (i, 0))],
        out_specs=pl.BlockSpec((tm, n), lambda i: (i, 0)),
        out_shape=jax.ShapeDtypeStruct((m, n), F32),
        compiler_params=_params("parallel"),
        name="matmul_residual",
    )(a, w, x)


def _rope_tables(pos, reps):
    half = ROT_DIM // 2
    inv = ROPE_THETA ** (-np.arange(half, dtype=np.float64) * (2.0 / ROT_DIM))
    ang = np.asarray(pos, np.float64)[:, None] * inv[None, :]
    cos, sin = np.cos(ang), np.sin(ang)
    n = ang.shape[0]
    ones = np.ones((n, HEAD_DIM - ROT_DIM))
    zeros = np.zeros((n, HEAD_DIM - ROT_DIM))
    z8 = np.zeros((n, half))
    c = np.concatenate([cos, cos, ones], axis=1)
    sa = np.concatenate([-sin, z8, zeros], axis=1)
    sb = np.concatenate([z8, sin, zeros], axis=1)
    return tuple(np.repeat(np.tile(t, (1, LANES // HEAD_DIM)), reps, axis=0).astype(np.float32)
                 for t in (c, sa, sb))


def _split_w_in(w_in):
    s2 = SSM_WIDTH + CONV_DIM
    s3 = s2 + SSM_HEADS
    assert s2 == COL_Q and COL_Q + (w_in.shape[1] - s3) == COL_DT
    w_dt = jnp.pad(w_in[:, s2:s3], ((0, 0), (0, LANES - SSM_HEADS)))
    return (w_in[:, :s2].astype(BF16), w_in[:, s3:].astype(BF16), w_dt.astype(BF16))


def _pad_lanes(v, width=LANES):
    return jnp.pad(v, (0, width - v.shape[0]))[None, :]


def kernel(x_prompt, x_sample, mem_prompt, state_ssm, state_conv, cache_swa_k, cache_swa_v,
           cache_mem_k, cache_mem_v, norm_mix, w_in, conv_w, conv_b, dt_bias, a_log, d_skip,
           gate_norm, sinks, w_out, norm_mem, norm_x, w_xq, w_xk, w_xv, w_xo, norm_ffn,
           w_gate, w_up, w_down, norm_final):
    bp, tp, _ = x_prompt.shape
    bs, ts, _ = x_sample.shape
    assert ts == DEC_T and tp % CHUNK == 0 and bs % SEQ_TILE == 0

    w_in_p = _split_w_in(w_in[0])
    w_out_b = w_out[0].astype(BF16)
    w_xq_b = w_xq[0].astype(BF16)
    w_xkv_b = (w_xk[0].astype(BF16), w_xv[0].astype(BF16))
    w_xo_b = w_xo[0].astype(BF16)
    w_gate_b, w_up_b, w_down_b = (w[0].astype(BF16) for w in (w_gate, w_up, w_down))
    row = lambda v: v.reshape(1, -1)
    cw = jnp.concatenate([conv_w[0][:, :SSM_WIDTH], conv_w[0][:, SSM_WIDTH:]], axis=1)
    cbias = row(conv_b[0])
    dtb = _pad_lanes(dt_bias[0])
    alog = _pad_lanes(a_log[0])
    dskip_e = row(jnp.repeat(d_skip[0], SSM_HEADDIM))
    dtb_e = row(jnp.repeat(dt_bias[0], SSM_HEADDIM))
    alog_e = row(jnp.repeat(a_log[0], SSM_HEADDIM))
    gn = row(gate_norm[0])
    tril = jnp.asarray(np.tril(np.ones((CHUNK, CHUNK), np.float32)), BF16)
    expand = jnp.asarray(np.arange(LANES)[:, None] == (np.arange(SSM_WIDTH)[None, :] // SSM_HEADDIM),
                         BF16)

    xp = x_prompt.reshape(bp * tp, D_MODEL)
    proj_p = norm_matmul(xp, row(norm_mix[0]), w_in_p, tm=512, tn=512)
    y_p, p_ssm = ssd_prompt(proj_p, cw, cbias, dtb, alog, dskip_e, gn, tril, expand,
                            batch=bp, seq=tp)
    cos_p, sa_p, sb_p = _rope_tables(np.arange(tp), 1)
    att_p, p_k = swa_prompt(proj_p, sinks[0], cos_p, sa_p, sb_p, batch=bp, seq=tp)
    x1_p, qx_p = out_proj(y_p, att_p, xp, w_out_b, row(norm_x[0]), w_xq_b, tm=512)
    mkv = norm_matmul(mem_prompt.reshape(bp * N_MEM, D_MODEL), row(norm_mem[0]), w_xkv_b,
                      tm=bp * N_MEM, tn=512)
    x2_p = xattn_prompt(qx_p, mkv, x1_p, w_xo_b, batch=bp, seq=tp, tq=512)
    y_prompt = ffn(x2_p, row(norm_ffn[0]), w_gate_b, w_up_b, w_down_b, row(norm_final),
                   tm=1024, tf=512).reshape(bp, tp, D_MODEL)

    proj_p3 = proj_p.reshape(bp, tp, PROJ_WIDTH)
    tail = proj_p3[:, tp - (CONV_WIDTH - 1):, :]
    p_conv = jnp.concatenate([tail[:, :, COL_X:COL_X + SSM_WIDTH],
                              tail[:, :, COL_BC:COL_BC + BC_WIDTH]], axis=-1)
    p_v = proj_p3[:, tp - WINDOW:, COL_KV + KV_WIDTH:COL_KV + 2 * KV_WIDTH]
    p_mk = mkv[:, :X_WIDTH].reshape(bp, N_MEM, X_HEADS, X_HEAD_DIM)
    p_mv = mkv[:, X_WIDTH:].reshape(bp, N_MEM, X_HEADS, X_HEAD_DIM)

    xs = x_sample.transpose(1, 0, 2).reshape(ts * bs, D_MODEL)
    proj_s = norm_matmul(xs, row(norm_mix[0]), w_in_p, tm=ts * bs, tn=512)
    proj_s3 = proj_s.reshape(ts, bs, PROJ_WIDTH)
    y_s, s_ssm, s_conv = ssd_sample(
        proj_s3, state_conv[0].transpose(1, 0, 2),
        state_ssm[0].reshape(bs, SSM_WIDTH, D_STATE), cw, cbias, dtb_e, alog_e, dskip_e, gn,
        expand, batch=bs)

    rep = ATT_HEADS // KV_HEADS
    q_s = proj_s3[:, :, COL_Q:COL_Q + ATT_WIDTH].reshape(ts, bs, KV_HEADS, rep, HEAD_DIM)
    q_s = q_s.transpose(1, 2, 0, 3, 4)
    zq = jnp.zeros_like(q_s[:, 0])
    q_bd = jnp.stack([jnp.concatenate([q_s[:, 0], zq], axis=-1),
                      jnp.concatenate([zq, q_s[:, 1]], axis=-1)], axis=1)
    q_bd = q_bd.reshape(bs, KV_HEADS * ts * rep, KV_WIDTH)
    kv_s = proj_s3[:, :, COL_KV:COL_KV + 2 * KV_WIDTH].transpose(1, 0, 2)
    kv_s = jnp.pad(kv_s, ((0, 0), (0, SUBLANES - ts), (0, 0)))
    pos_s = PAST_LEN + np.arange(ts)
    ktabs = _rope_tables(PAST_LEN + np.arange(SUBLANES), 1)
    qtabs = tuple(np.tile(t, (KV_HEADS, 1)) for t in _rope_tables(pos_s, rep))
    sink_rows = jnp.broadcast_to(
        jnp.broadcast_to(sinks[0].reshape(KV_HEADS, 1, rep), (KV_HEADS, ts, rep)).reshape(-1, 1),
        (KV_HEADS * ts * rep, LANES))
    o_bd, s_k, s_v = swa_sample(
        q_bd, kv_s[:, :, :KV_WIDTH], kv_s[:, :, KV_WIDTH:],
        cache_swa_k[0].reshape(bs, WINDOW, KV_WIDTH), cache_swa_v[0].reshape(bs, WINDOW, KV_WIDTH),
        sink_rows, qtabs, ktabs, batch=bs)
    o_bd = o_bd.reshape(bs, KV_HEADS, ts, rep, KV_HEADS, HEAD_DIM)
    att_s = jnp.stack([o_bd[:, g, :, :, g, :] for g in range(KV_HEADS)], axis=2)
    att_s = att_s.transpose(1, 0, 2, 3, 4).reshape(ts * bs, ATT_WIDTH).astype(BF16)

    x1_s, qx_s = out_proj(y_s.reshape(ts * bs, SSM_WIDTH), att_s, xs, w_out_b, row(norm_x[0]),
                          w_xq_b, tm=ts * bs)
    qx_s8 = jnp.pad(qx_s.astype(F32).reshape(ts, bs, X_WIDTH).transpose(1, 0, 2),
                    ((0, 0), (0, SUBLANES - ts), (0, 0)))
    o_x = xattn_sample(qx_s8, cache_mem_k.reshape(bs, N_MEM * X_HEADS, X_HEAD_DIM),
                       cache_mem_v.reshape(bs, N_MEM * X_HEADS, X_HEAD_DIM), batch=bs)
    o_x = o_x.reshape(bs, X_HEADS, SUBLANES, X_HEAD_DIM)[:, :, :ts]
    o_x = o_x.transpose(2, 0, 1, 3).reshape(ts * bs, X_WIDTH).astype(BF16)
    x2_s = matmul_residual(o_x, w_xo_b, x1_s, tm=ts * bs)
    y_s_out = ffn(x2_s, row(norm_ffn[0]), w_gate_b, w_up_b, w_down_b, row(norm_final),
                  tm=ts * bs, tf=512)
    y_sample = y_s_out.reshape(ts, bs, D_MODEL).transpose(1, 0, 2)

    return (y_prompt, y_sample,
            p_ssm.reshape(1, bp, SSM_HEADS, SSM_HEADDIM, D_STATE), p_conv[None],
            p_k.reshape(1, bp, WINDOW, KV_HEADS, HEAD_DIM),
            p_v.reshape(1, bp, WINDOW, KV_HEADS, HEAD_DIM),
            p_mk[None], p_mv[None],
            s_ssm.reshape(1, bs, SSM_HEADS, SSM_HEADDIM, D_STATE),
            s_conv.transpose(1, 0, 2)[None],
            s_k.reshape(1, bs, WINDOW, KV_HEADS, HEAD_DIM),
            s_v.reshape(1, bs, WINDOW, KV_HEADS, HEAD_DIM))
```

```python
import functools

import jax
import jax.numpy as jnp
import numpy as np
from jax import lax
from jax.experimental import pallas as pl
from jax.experimental.pallas import tpu as pltpu

F32 = jnp.float32
BF16 = jnp.bfloat16

D_MODEL = 2048
SSM_WIDTH = 1024
SSM_HEADDIM = 64
SSM_HEADS = 16
SSM_GROUPS = 2
D_STATE = 128
CONV_WIDTH = 4
CONV_DIM = SSM_WIDTH + 2 * SSM_GROUPS * D_STATE
BC_WIDTH = 2 * SSM_GROUPS * D_STATE
ATT_WIDTH = 1024
HEAD_DIM = 64
ATT_HEADS = 16
KV_HEADS = 2
KV_WIDTH = KV_HEADS * HEAD_DIM
WINDOW = 128
CHUNK = 128
ROT_DIM = 16
ROPE_THETA = 500000.0
PAST_LEN = 16384
N_MEM = 256
X_HEADS = 4
X_HEAD_DIM = 128
X_WIDTH = X_HEADS * X_HEAD_DIM
EPS = 1e-5

LANES = 128
SUBLANES = 8
VMEM_LIMIT_BYTES = 56 * 1024 * 1024

PROJ_WIDTH = 3968
COL_Z = 0
COL_X = 1024
COL_BC = 2048
COL_Q = 2560
COL_KV = 3584
COL_DT = 3840
Q_BLOCK = 512

NEG_INF = float("-inf")


def _params(*semantics):
    return pltpu.CompilerParams(dimension_semantics=semantics,
                                vmem_limit_bytes=VMEM_LIMIT_BYTES)


def _rmsnorm(x, gain):
    ms = jnp.mean(x * x, axis=-1, keepdims=True)
    return x * lax.rsqrt(ms + EPS) * gain


def _silu(x):
    return x * jax.nn.sigmoid(x)


def _softplus(x):
    return jnp.maximum(x, 0.0) + jnp.log1p(jnp.exp(-jnp.abs(x)))


def _split_bf16(v, parts):
    out = []
    rem = v
    for _ in range(parts - 1):
        p = rem.astype(BF16)
        out.append(p)
        rem = rem - p.astype(F32)
    out.append(rem.astype(BF16))
    return out


def _dot(a, b):
    return jnp.dot(a, b, preferred_element_type=F32)


def _dot_nt(a, b):
    return lax.dot_general(a, b, (((1,), (1,)), ((), ())), preferred_element_type=F32)


def _dot_tn(a, b):
    return lax.dot_general(a, b, (((0,), (0,)), ((), ())), preferred_element_type=F32)


def _select_matmul(v, sel, parts):
    acc = None
    for p in _split_bf16(v, parts):
        t = _dot(p, sel)
        acc = t if acc is None else acc + t
    return acc


def _norm_matmul_kernel(x_ref, g_ref, *refs, tn):
    w_refs, o_ref = refs[:-1], refs[-1]
    h = _rmsnorm(x_ref[...], g_ref[...]).astype(BF16)
    col = 0
    for w_ref in w_refs:
        width = w_ref.shape[1]
        for j in range(0, width, tn):
            n = min(tn, width - j)
            o_ref[:, col + j:col + j + n] = _dot(h, w_ref[:, j:j + n]).astype(o_ref.dtype)
        col += width


def norm_matmul(x, gain, ws, *, tm, tn, out_dtype=F32):
    m, k = x.shape
    n = sum(w.shape[1] for w in ws)
    return pl.pallas_call(
        functools.partial(_norm_matmul_kernel, tn=tn),
        grid=(m // tm,),
        in_specs=[pl.BlockSpec((tm, k), lambda i: (i, 0)),
                  pl.BlockSpec((1, k), lambda i: (0, 0))]
                 + [pl.BlockSpec(w.shape, lambda i: (0, 0), pipeline_mode=pl.Buffered(1))
                    for w in ws],
        out_specs=pl.BlockSpec((tm, n), lambda i: (i, 0)),
        out_shape=jax.ShapeDtypeStruct((m, n), out_dtype),
        compiler_params=_params("parallel"),
        name="norm_matmul",
    )(x, gain, *ws)


BF16_SUBLANES = 16


def _with_weight_casts(body, n_in, n_out, n_cast):
    def kernel(*refs):
        ins = refs[:n_in]
        cast_in = refs[n_in:n_in + n_cast]
        outs = refs[n_in + n_cast:n_in + n_cast + n_out]
        cast_out = refs[n_in + n_cast + n_out:n_in + 2 * n_cast + n_out]
        scratch = refs[n_in + 2 * n_cast + n_out:]
        for src, dst in zip(cast_in, cast_out):
            dst[...] = src[...].astype(dst.dtype)
        body(*ins, *outs, *scratch)
    return kernel


def _cast_specs(ws, batch, steps):
    specs, shapes = [], []
    for w in ws:
        r, c = w.shape
        assert r % (steps * BF16_SUBLANES) == 0 and c % (batch * LANES) == 0, w.shape
        specs.append(pl.BlockSpec((r // steps, c // batch), lambda b, i: (i, b)))
        shapes.append(jax.ShapeDtypeStruct(w.shape, BF16))
    return specs, shapes


def _ssd_prompt_kernel(z_ref, x_ref, bc_ref, dt_ref, cw_ref, cb_ref, dtb_ref, alog_ref,
                       dskip_ref, gn_ref, tril_ref, exp_ref,
                       y_ref, state_ref, carry_ref, st_ref):
    c = pl.program_id(1)
    nc = pl.num_programs(1)
    t = CHUNK

    @pl.when(c == 0)
    def _():
        carry_ref[...] = jnp.zeros_like(carry_ref)
        st_ref[...] = jnp.zeros_like(st_ref)

    x_new = jnp.concatenate([x_ref[...], bc_ref[...]], axis=1)
    cat = jnp.concatenate([carry_ref[...], x_new], axis=0)
    acc = jnp.broadcast_to(cb_ref[...], (t, CONV_DIM))
    for k in range(CONV_WIDTH):
        back = CONV_WIDTH - 1 - k
        tap = x_new if back == 0 else pltpu.roll(cat, back, axis=0)[SUBLANES:SUBLANES + t]
        acc = acc + tap * cw_ref[k:k + 1, :]
    xbc = _silu(acc)
    carry_ref[...] = x_new[t - SUBLANES:t]

    xs = xbc[:, 0:SSM_WIDTH]
    xs_b = xs.astype(BF16)
    b_mat = [xbc[:, SSM_WIDTH + g * D_STATE:SSM_WIDTH + (g + 1) * D_STATE].astype(BF16)
             for g in range(SSM_GROUPS)]
    c_off = SSM_WIDTH + SSM_GROUPS * D_STATE
    c_mat = [xbc[:, c_off + g * D_STATE:c_off + (g + 1) * D_STATE].astype(BF16)
             for g in range(SSM_GROUPS)]

    dt = _softplus(dt_ref[...] + dtb_ref[...])
    a = -jnp.exp(alog_ref[...])
    da = dt * a
    cs = _select_matmul_left(tril_ref[...], da)
    cs_t = cs.T
    dt_t = dt.T
    cs_last = cs[t - 1:t, :]
    exp_cs = jnp.exp(cs)
    to_end = jnp.exp(cs_last - cs) * dt
    sel = exp_ref[...]
    exp_cs_e = _select_matmul(exp_cs, sel, 2)
    to_end_e = _select_matmul(to_end, sel, 2)

    row = lax.broadcasted_iota(jnp.int32, (t, 2 * t), 0)
    col = lax.broadcasted_iota(jnp.int32, (t, 2 * t), 1)
    causal = jnp.bitwise_and(col, t - 1) <= row
    lane = lax.broadcasted_iota(jnp.int32, (t, LANES), 1)
    left = lane < SSM_HEADDIM

    cb = [_dot_nt(c_mat[g], b_mat[g]) for g in range(SSM_GROUPS)]
    cb2 = [jnp.concatenate([m, m], axis=1) for m in cb]

    y_parts = []
    for j in range(SSM_HEADS // 2):
        g = (2 * j) // (SSM_HEADS // SSM_GROUPS)
        h0, h1 = 2 * j, 2 * j + 1
        colv = jnp.concatenate([jnp.broadcast_to(cs[:, h0:h0 + 1], (t, t)),
                                jnp.broadcast_to(cs[:, h1:h1 + 1], (t, t))], axis=1)
        rowv = jnp.concatenate([jnp.broadcast_to(cs_t[h0:h0 + 1, :], (t, t)),
                                jnp.broadcast_to(cs_t[h1:h1 + 1, :], (t, t))], axis=1)
        dtr = jnp.concatenate([jnp.broadcast_to(dt_t[h0:h0 + 1, :], (t, t)),
                               jnp.broadcast_to(dt_t[h1:h1 + 1, :], (t, t))], axis=1)
        decay = jnp.where(causal, jnp.exp(colv - rowv), 0.0)
        wts = (cb2[g] * decay * dtr).astype(BF16)
        xp = xs_b[:, j * LANES:(j + 1) * LANES]
        zero = jnp.zeros_like(xp)
        rhs = jnp.concatenate([jnp.where(left, xp, zero), jnp.where(left, zero, xp)], axis=0)
        y_parts.append(_dot(wts, rhs))
    y_diag = jnp.concatenate(y_parts, axis=1)

    half = SSM_WIDTH // SSM_GROUPS
    st = st_ref[...]
    st_b = st.astype(BF16)
    y_off = jnp.concatenate(
        [_dot(c_mat[g], st_b[:, g * half:(g + 1) * half]) for g in range(SSM_GROUPS)], axis=1)
    xw = (xs * to_end_e).astype(BF16)
    upd = jnp.concatenate(
        [_dot_tn(b_mat[g], xw[:, g * half:(g + 1) * half]) for g in range(SSM_GROUPS)], axis=1)
    st_new = st * exp_cs_e[t - 1:t, :] + upd
    st_ref[...] = st_new

    y = y_diag + y_off * exp_cs_e + xs * dskip_ref[...]
    gated = y * _silu(z_ref[...])
    outs = []
    for g in range(SSM_GROUPS):
        outs.append(_rmsnorm(gated[:, g * half:(g + 1) * half], gn_ref[:, g * half:(g + 1) * half]))
    y_ref[...] = jnp.concatenate(outs, axis=1).astype(y_ref.dtype)

    @pl.when(c == nc - 1)
    def _():
        state_ref[0] = st_new.T


def _select_matmul_left(sel, v):
    acc = None
    for p in _split_bf16(v, 3):
        t = _dot(sel, p)
        acc = t if acc is None else acc + t
    return acc


def ssd_prompt(proj, conv_w, conv_b, dtb, alog, dskip_e, gate_norm, tril, expand, casts, *,
               batch, seq):
    nc = seq // CHUNK
    rows = lambda b, c: b * nc + c
    const = lambda b, c: (0, 0)
    in_specs = [
        pl.BlockSpec((CHUNK, SSM_WIDTH), lambda b, c: (rows(b, c), COL_Z // SSM_WIDTH)),
        pl.BlockSpec((CHUNK, SSM_WIDTH), lambda b, c: (rows(b, c), COL_X // SSM_WIDTH)),
        pl.BlockSpec((CHUNK, BC_WIDTH), lambda b, c: (rows(b, c), COL_BC // BC_WIDTH)),
        pl.BlockSpec((CHUNK, LANES), lambda b, c: (rows(b, c), COL_DT // LANES)),
        pl.BlockSpec((CONV_WIDTH, CONV_DIM), const),
        pl.BlockSpec((1, CONV_DIM), const),
        pl.BlockSpec((1, LANES), const),
        pl.BlockSpec((1, LANES), const),
        pl.BlockSpec((1, SSM_WIDTH), const),
        pl.BlockSpec((1, SSM_WIDTH), const),
        pl.BlockSpec((CHUNK, CHUNK), const),
        pl.BlockSpec((LANES, SSM_WIDTH), const),
    ]
    out_specs = [pl.BlockSpec((CHUNK, SSM_WIDTH), lambda b, c: (rows(b, c), 0)),
                 pl.BlockSpec((1, SSM_WIDTH, D_STATE), lambda b, c: (b, 0, 0))]
    out_shape = [jax.ShapeDtypeStruct((batch * seq, SSM_WIDTH), BF16),
                 jax.ShapeDtypeStruct((batch, SSM_WIDTH, D_STATE), F32)]
    cast_specs, cast_shapes = _cast_specs(casts, batch, nc)
    y, state, *cast_out = pl.pallas_call(
        _with_weight_casts(_ssd_prompt_kernel, len(in_specs), len(out_specs), len(casts)),
        grid=(batch, nc),
        in_specs=in_specs + cast_specs,
        out_specs=out_specs + cast_specs,
        out_shape=out_shape + cast_shapes,
        scratch_shapes=[pltpu.VMEM((SUBLANES, CONV_DIM), F32),
                        pltpu.VMEM((D_STATE, SSM_WIDTH), F32)],
        compiler_params=_params("parallel", "arbitrary"),
        name="ssd_prompt",
    )(proj, proj, proj, proj, conv_w, conv_b, dtb, alog, dskip_e, gate_norm, tril, expand, *casts)
    return y, state, cast_out


def _rope(x, cos, sin_a, sin_b):
    parts = []
    for i in range(x.shape[1] // LANES):
        xb = x[:, i * LANES:(i + 1) * LANES]
        up = pltpu.roll(xb, LANES - ROT_DIM // 2, axis=1)
        dn = pltpu.roll(xb, ROT_DIM // 2, axis=1)
        parts.append(xb * cos + up * sin_a + dn * sin_b)
    return parts[0] if len(parts) == 1 else jnp.concatenate(parts, axis=1)


def _block_diag_rows(a, a_swapped, first):
    lane = lax.broadcasted_iota(jnp.int32, a.shape, 1)
    left = lane < HEAD_DIM
    zero = jnp.zeros_like(a)
    if first == 0:
        top, bot = jnp.where(left, a, zero), jnp.where(left, zero, a_swapped)
    else:
        top, bot = jnp.where(left, a_swapped, zero), jnp.where(left, zero, a)
    return jnp.concatenate([top, bot], axis=0)


def _swa_prompt_kernel(sink_ref, qlo_ref, qhi_ref, kv_ref, cos_ref, sa_ref, sb_ref,
                       att_ref, k_ref, kprev_ref, vprev_ref, s_ref, p_ref):
    blk = pl.program_id(1)
    w = WINDOW
    pairs = ATT_HEADS // 2

    @pl.when(blk == 0)
    def _():
        kprev_ref[...] = jnp.zeros_like(kprev_ref)
        vprev_ref[...] = jnp.zeros_like(vprev_ref)

    cos, sa, sb = cos_ref[...], sa_ref[...], sb_ref[...]
    k_cur = _rope(kv_ref[:, 0:KV_WIDTH], cos, sa, sb)
    v_cur = kv_ref[:, KV_WIDTH:2 * KV_WIDTH]
    k_ref[0] = k_cur
    q = jnp.concatenate([qlo_ref[...], qhi_ref[...]], axis=1)
    q = (_rope(q, cos, sa, sb) * (HEAD_DIM ** -0.5)).astype(BF16)

    k_cat = jnp.concatenate([kprev_ref[...], k_cur], axis=0)
    v_cat = jnp.concatenate([vprev_ref[...], v_cur], axis=0)
    k_sw = pltpu.roll(k_cat, HEAD_DIM, axis=1)
    v_sw = pltpu.roll(v_cat, HEAD_DIM, axis=1)
    kbd = [_block_diag_rows(k_cat, k_sw, g).astype(BF16) for g in range(KV_HEADS)]
    vbd_t = [_block_diag_rows(v_cat, v_sw, g).T.astype(BF16) for g in range(KV_HEADS)]

    for j in range(pairs):
        g = (2 * j) // (ATT_HEADS // KV_HEADS)
        s_ref[j] = _dot_nt(kbd[g], q[:, j * LANES:(j + 1) * LANES])

    key = lax.broadcasted_iota(jnp.int32, (w, w), 0)
    qry = lax.broadcasted_iota(jnp.int32, (w, w), 1)
    own_valid = key <= qry
    has_prev = blk > 0
    for j in range(pairs):
        for hh in range(2):
            base = hh * 2 * w
            sink = sink_ref[2 * j + hh]
            prev = jnp.where(has_prev, s_ref[j, base:base + w, :], NEG_INF)
            s = jnp.where(own_valid, s_ref[j, base + w:base + 2 * w, :], prev)
            m = jnp.maximum(jnp.max(s, axis=0, keepdims=True), sink)
            e = jnp.exp(s - m)
            den = jnp.sum(e, axis=0, keepdims=True) + jnp.exp(sink - m)
            p = e * (1.0 / den)
            p_ref[j, base:base + w, :] = jnp.where(own_valid, 0.0, p).astype(BF16)
            p_ref[j, base + w:base + 2 * w, :] = jnp.where(own_valid, p, 0.0).astype(BF16)

    for j in range(pairs):
        g = (2 * j) // (ATT_HEADS // KV_HEADS)
        o_t = _dot(vbd_t[g], p_ref[j])
        att_ref[:, j * LANES:(j + 1) * LANES] = o_t.T.astype(att_ref.dtype)

    kprev_ref[...] = k_cur
    vprev_ref[...] = v_cur


def swa_prompt(proj, sinks, cos, sin_a, sin_b, casts, *, batch, seq):
    nb = seq // WINDOW
    rows = lambda b, i: b * nb + i
    in_specs = [
        pl.BlockSpec(memory_space=pltpu.SMEM),
        pl.BlockSpec((WINDOW, Q_BLOCK), lambda b, i: (rows(b, i), COL_Q // Q_BLOCK)),
        pl.BlockSpec((WINDOW, Q_BLOCK), lambda b, i: (rows(b, i), COL_Q // Q_BLOCK + 1)),
        pl.BlockSpec((WINDOW, 2 * KV_WIDTH), lambda b, i: (rows(b, i), COL_KV // (2 * KV_WIDTH))),
        pl.BlockSpec((WINDOW, LANES), lambda b, i: (i, 0)),
        pl.BlockSpec((WINDOW, LANES), lambda b, i: (i, 0)),
        pl.BlockSpec((WINDOW, LANES), lambda b, i: (i, 0)),
    ]
    out_specs = [pl.BlockSpec((WINDOW, ATT_WIDTH), lambda b, i: (rows(b, i), 0)),
                 pl.BlockSpec((1, WINDOW, KV_WIDTH), lambda b, i: (b, 0, 0))]
    out_shape = [jax.ShapeDtypeStruct((batch * seq, ATT_WIDTH), BF16),
                 jax.ShapeDtypeStruct((batch, WINDOW, KV_WIDTH), F32)]
    cast_specs, cast_shapes = _cast_specs(casts, batch, nb)
    att, k_last, *cast_out = pl.pallas_call(
        _with_weight_casts(_swa_prompt_kernel, len(in_specs), len(out_specs), len(casts)),
        grid=(batch, nb),
        in_specs=in_specs + cast_specs,
        out_specs=out_specs + cast_specs,
        out_shape=out_shape + cast_shapes,
        scratch_shapes=[pltpu.VMEM((WINDOW, KV_WIDTH), F32),
                        pltpu.VMEM((WINDOW, KV_WIDTH), F32),
                        pltpu.VMEM((ATT_HEADS // 2, 4 * WINDOW, WINDOW), F32),
                        pltpu.VMEM((ATT_HEADS // 2, 4 * WINDOW, WINDOW), BF16)],
        compiler_params=_params("parallel", "arbitrary"),
        name="swa_prompt",
    )(sinks, proj, proj, proj, cos, sin_a, sin_b, *casts)
    return att, k_last, cast_out


def _out_proj_kernel(y_ref, a_ref, x_ref, wo_ref, gx_ref, wq_ref, x1_ref, q_ref):
    mix = _dot(y_ref[...], wo_ref[0:SSM_WIDTH, :]) + _dot(a_ref[...], wo_ref[SSM_WIDTH:, :])
    x1 = x_ref[...] + mix
    x1_ref[...] = x1
    h = _rmsnorm(x1, gx_ref[...]).astype(BF16)
    q_ref[...] = _dot(h, wq_ref[...]).astype(q_ref.dtype)


def out_proj(y, att, x, w_out, norm_x, w_xq, *, tm):
    m = x.shape[0]
    const = lambda i: (0, 0)
    return pl.pallas_call(
        _out_proj_kernel,
        grid=(m // tm,),
        in_specs=[pl.BlockSpec((tm, SSM_WIDTH), lambda i: (i, 0)),
                  pl.BlockSpec((tm, ATT_WIDTH), lambda i: (i, 0)),
                  pl.BlockSpec((tm, D_MODEL), lambda i: (i, 0)),
                  pl.BlockSpec((D_MODEL, D_MODEL), const),
                  pl.BlockSpec((1, D_MODEL), const),
                  pl.BlockSpec((D_MODEL, X_WIDTH), const)],
        out_specs=[pl.BlockSpec((tm, D_MODEL), lambda i: (i, 0)),
                   pl.BlockSpec((tm, X_WIDTH), lambda i: (i, 0))],
        out_shape=[jax.ShapeDtypeStruct((m, D_MODEL), F32),
                   jax.ShapeDtypeStruct((m, X_WIDTH), BF16)],
        compiler_params=_params("parallel"),
        name="out_proj",
    )(y, att, x, w_out, norm_x, w_xq)


def _softmax(s):
    m = jnp.max(s, axis=-1, keepdims=True)
    e = jnp.exp(s - m)
    return e / jnp.sum(e, axis=-1, keepdims=True)


def _xattn_prompt_kernel(q_ref, mk_ref, mv_ref, x1_ref, wo_ref, x2_ref):
    mk = mk_ref[...].astype(BF16)
    mv = mv_ref[...].astype(BF16)
    outs = []
    for h in range(X_HEADS):
        sl = slice(h * X_HEAD_DIM, (h + 1) * X_HEAD_DIM)
        s = _dot_nt(q_ref[:, sl], mk[:, sl]) * (X_HEAD_DIM ** -0.5)
        outs.append(_dot(_softmax(s).astype(BF16), mv[:, sl]).astype(BF16))
    o = jnp.concatenate(outs, axis=1)
    x2_ref[...] = x1_ref[...] + _dot(o, wo_ref[...])


def xattn_prompt(q, mkv, x1, w_xo, *, batch, seq, tq):
    nq = seq // tq
    return pl.pallas_call(
        _xattn_prompt_kernel,
        grid=(batch, nq),
        in_specs=[pl.BlockSpec((tq, X_WIDTH), lambda b, i: (b * nq + i, 0)),
                  pl.BlockSpec((N_MEM, X_WIDTH), lambda b, i: (b, 0)),
                  pl.BlockSpec((N_MEM, X_WIDTH), lambda b, i: (b, 1)),
                  pl.BlockSpec((tq, D_MODEL), lambda b, i: (b * nq + i, 0)),
                  pl.BlockSpec((X_WIDTH, D_MODEL), lambda b, i: (0, 0))],
        out_specs=pl.BlockSpec((tq, D_MODEL), lambda b, i: (b * nq + i, 0)),
        out_shape=jax.ShapeDtypeStruct(x1.shape, F32),
        compiler_params=_params("parallel", "parallel"),
        name="xattn_prompt",
    )(q, mkv, mkv, x1, w_xo)


def _ffn_kernel(x_ref, gn_ref, wg_ref, wu_ref, wd_ref, gf_ref, o_ref, h_ref):
    f = pl.program_id(1)

    @pl.when(f == 0)
    def _():
        x = x_ref[...]
        h_ref[...] = _rmsnorm(x, gn_ref[...]).astype(BF16)
        o_ref[...] = x

    h = h_ref[...]
    act = (_silu(_dot(h, wg_ref[...])) * _dot(h, wu_ref[...])).astype(BF16)
    o_ref[...] += _dot(act, wd_ref[...])

    @pl.when(f == pl.num_programs(1) - 1)
    def _():
        o_ref[...] = _rmsnorm(o_ref[...], gf_ref[...])


def ffn(x, norm_ffn, w_gate, w_up, w_down, norm_final, *, tm, tf):
    m = x.shape[0]
    d_ff = w_gate.shape[1]
    return pl.pallas_call(
        _ffn_kernel,
        grid=(m // tm, d_ff // tf),
        in_specs=[pl.BlockSpec((tm, D_MODEL), lambda i, f: (i, 0)),
                  pl.BlockSpec((1, D_MODEL), lambda i, f: (0, 0)),
                  pl.BlockSpec((D_MODEL, tf), lambda i, f: (0, f)),
                  pl.BlockSpec((D_MODEL, tf), lambda i, f: (0, f)),
                  pl.BlockSpec((tf, D_MODEL), lambda i, f: (f, 0)),
                  pl.BlockSpec((1, D_MODEL), lambda i, f: (0, 0))],
        out_specs=pl.BlockSpec((tm, D_MODEL), lambda i, f: (i, 0)),
        out_shape=jax.ShapeDtypeStruct((m, D_MODEL), F32),
        scratch_shapes=[pltpu.VMEM((tm, D_MODEL), BF16)],
        compiler_params=_params("parallel", "arbitrary"),
        name="ffn",
    )(x, norm_ffn, w_gate, w_up, w_down, norm_final)


SEQ_TILE = SUBLANES
DEC_T = 4


def _ssd_sample_kernel(z_ref, x_ref, bc_ref, dt_ref, conv0_ref, h0_ref, cw_ref, cb_ref,
                       dtbe_ref, aloge_ref, dskip_ref, gn_ref, exp_ref,
                       y_ref, hnew_ref, conv_ref):
    nt, sb = DEC_T, SEQ_TILE
    half = SSM_WIDTH // SSM_GROUPS

    xin = [conv0_ref[j] for j in range(CONV_WIDTH - 1)]
    xin += [jnp.concatenate([x_ref[u], bc_ref[u]], axis=1) for u in range(nt)]
    for j in range(CONV_WIDTH - 1):
        conv_ref[j] = xin[nt + j]
    xbc = []
    for u in range(nt):
        acc = jnp.broadcast_to(cb_ref[...], (sb, CONV_DIM))
        for k in range(CONV_WIDTH):
            acc = acc + xin[u + k] * cw_ref[k:k + 1, :]
        xbc.append(_silu(acc))
    xs = [v[:, 0:SSM_WIDTH] for v in xbc]
    c_off = SSM_WIDTH + SSM_GROUPS * D_STATE
    b_rows = [v[:, SSM_WIDTH:c_off] for v in xbc]
    c_rows = [v[:, c_off:CONV_DIM] for v in xbc]

    dt_raw = jnp.concatenate([dt_ref[u] for u in range(nt)], axis=0)
    dt_e = _softplus(_select_matmul(dt_raw, exp_ref[...], 3) + dtbe_ref[...])
    da_e = dt_e * -jnp.exp(aloge_ref[...])
    dts = [dt_e[u * sb:(u + 1) * sb] for u in range(nt)]
    cs = []
    for u in range(nt):
        d = da_e[u * sb:(u + 1) * sb]
        cs.append(d if u == 0 else cs[-1] + d)

    lane = lax.broadcasted_iota(jnp.int32, (sb, SSM_WIDTH), 1)
    first_group = lane < half

    def group_bcast(v0, v1):
        return jnp.where(first_group, v0, v1)

    y = []
    for u in range(nt):
        acc = None
        for s in range(u + 1):
            prod = c_rows[u] * b_rows[s]
            cbv = [jnp.sum(prod[:, g * D_STATE:(g + 1) * D_STATE], axis=-1, keepdims=True)
                   for g in range(SSM_GROUPS)]
            coef = group_bcast(cbv[0], cbv[1]) * jnp.exp(cs[u] - cs[s]) * dts[s]
            term = coef * xs[s]
            acc = term if acc is None else acc + term
        y.append(acc)

    c_stack = jnp.concatenate(c_rows, axis=0).astype(BF16)
    to_end = [jnp.exp(cs[nt - 1] - cs[u]) * dts[u] for u in range(nt)]
    xw = [xs[u] * to_end[u] for u in range(nt)]
    dec_parts = [p.astype(F32) for p in _split_bf16(jnp.exp(cs[nt - 1]), 3)]
    pad_rows = jnp.zeros((sb, SSM_WIDTH), F32)
    lhs_t = jnp.concatenate(xw + dec_parts + [pad_rows], axis=0).T.astype(BF16)
    ones = jnp.ones((sb, D_STATE), F32)
    zeros = jnp.zeros((sb, D_STATE), F32)
    rhs = []
    for g in range(SSM_GROUPS):
        bg = [v[:, g * D_STATE:(g + 1) * D_STATE] for v in b_rows]
        left = jnp.concatenate(bg + [zeros] * 4, axis=0)
        right = jnp.concatenate([zeros] * nt + [ones] * 3 + [zeros], axis=0)
        rhs.append(jnp.concatenate([left, right], axis=1))
    krow = jnp.bitwise_and(lax.broadcasted_iota(jnp.int32, (2 * nt * sb, 2 * D_STATE), 0), sb - 1)
    yrow = jnp.bitwise_and(lax.broadcasted_iota(jnp.int32, (nt * sb, SSM_WIDTH), 0), sb - 1)

    y_off = jnp.zeros((nt * sb, SSM_WIDTH), F32)
    for i in range(sb):
        h0 = h0_ref[i]
        h0_b = h0.astype(BF16)
        off = jnp.concatenate(
            [_dot_nt(c_stack[:, g * D_STATE:(g + 1) * D_STATE], h0_b[g * half:(g + 1) * half])
             for g in range(SSM_GROUPS)], axis=1)
        y_off = y_off + jnp.where(yrow == i, off, 0.0)
        new = []
        for g in range(SSM_GROUPS):
            r = jnp.where(krow == i, rhs[g], 0.0).astype(BF16)
            res = _dot(lhs_t[g * half:(g + 1) * half], r)
            new.append(h0[g * half:(g + 1) * half] * res[:, D_STATE:] + res[:, :D_STATE])
        hnew_ref[i] = jnp.concatenate(new, axis=0)

    for u in range(nt):
        yu = y[u] + y_off[u * sb:(u + 1) * sb] * jnp.exp(cs[u]) + xs[u] * dskip_ref[...]
        gated = yu * _silu(z_ref[u])
        outs = [_rmsnorm(gated[:, g * half:(g + 1) * half], gn_ref[:, g * half:(g + 1) * half])
                for g in range(SSM_GROUPS)]
        y_ref[u] = jnp.concatenate(outs, axis=1).astype(y_ref.dtype)


def ssd_sample(proj, conv0, h0, conv_w, conv_b, dtb_e, alog_e, dskip_e, gate_norm, expand, *,
               batch):
    sb = SEQ_TILE
    const = lambda i: (0, 0)
    return pl.pallas_call(
        _ssd_sample_kernel,
        grid=(batch // sb,),
        in_specs=[
            pl.BlockSpec((DEC_T, sb, SSM_WIDTH), lambda i: (0, i, COL_Z // SSM_WIDTH)),
            pl.BlockSpec((DEC_T, sb, SSM_WIDTH), lambda i: (0, i, COL_X // SSM_WIDTH)),
            pl.BlockSpec((DEC_T, sb, BC_WIDTH), lambda i: (0, i, COL_BC // BC_WIDTH)),
            pl.BlockSpec((DEC_T, sb, LANES), lambda i: (0, i, COL_DT // LANES)),
            pl.BlockSpec((CONV_WIDTH - 1, sb, CONV_DIM), lambda i: (0, i, 0)),
            pl.BlockSpec((sb, SSM_WIDTH, D_STATE), lambda i: (i, 0, 0)),
            pl.BlockSpec((CONV_WIDTH, CONV_DIM), const),
            pl.BlockSpec((1, CONV_DIM), const),
            pl.BlockSpec((1, SSM_WIDTH), const),
            pl.BlockSpec((1, SSM_WIDTH), const),
            pl.BlockSpec((1, SSM_WIDTH), const),
            pl.BlockSpec((1, SSM_WIDTH), const),
            pl.BlockSpec((LANES, SSM_WIDTH), const),
        ],
        out_specs=[pl.BlockSpec((DEC_T, sb, SSM_WIDTH), lambda i: (0, i, 0)),
                   pl.BlockSpec((sb, SSM_WIDTH, D_STATE), lambda i: (i, 0, 0)),
                   pl.BlockSpec((CONV_WIDTH - 1, sb, CONV_DIM), lambda i: (0, i, 0))],
        out_shape=[jax.ShapeDtypeStruct((DEC_T, batch, SSM_WIDTH), BF16),
                   jax.ShapeDtypeStruct((batch, SSM_WIDTH, D_STATE), F32),
                   jax.ShapeDtypeStruct((CONV_WIDTH - 1, batch, CONV_DIM), F32)],
        compiler_params=_params("parallel"),
        name="ssd_sample",
    )(proj, proj, proj, proj, conv0, h0, conv_w, conv_b, dtb_e, alog_e, dskip_e, gate_norm, expand)


def _swa_sample_kernel(q_ref, kn_ref, vn_ref, kc_ref, vc_ref, sink_ref, qcos_ref, qsa_ref,
                       qsb_ref, kcos_ref, ksa_ref, ksb_ref, o_ref, ko_ref, vo_ref):
    w = WINDOW
    nt = DEC_T
    nq = KV_HEADS * nt * (ATT_HEADS // KV_HEADS)
    def token_of(rows):
        return jnp.bitwise_and(jnp.right_shift(rows, 3), nt - 1)

    row_t = token_of(lax.broadcasted_iota(jnp.int32, (nq, w), 0))
    col = lax.broadcasted_iota(jnp.int32, (nq, w), 1)
    mask_c = col > row_t
    col_n = lax.broadcasted_iota(jnp.int32, (nq, SUBLANES), 1)
    row_n = token_of(lax.broadcasted_iota(jnp.int32, (nq, SUBLANES), 0))
    mask_n = col_n <= row_n
    sink = sink_ref[...][:, 0:1]

    def body(i, carry):
        q = (_rope(q_ref[i], qcos_ref[...], qsa_ref[...], qsb_ref[...])
             * (HEAD_DIM ** -0.5)).astype(BF16)
        kn8 = _rope(kn_ref[i], kcos_ref[...], ksa_ref[...], ksb_ref[...])
        vn8 = vn_ref[i]
        kn = kn8[0:nt]
        vn = vn8[0:nt]
        kc = kc_ref[i]
        vc = vc_ref[i]
        s_c = jnp.where(mask_c, _dot_nt(q, kc.astype(BF16)), NEG_INF)
        s_n = jnp.where(mask_n, _dot_nt(q, kn8.astype(BF16)), NEG_INF)
        m = jnp.maximum(jnp.maximum(jnp.max(s_c, axis=-1, keepdims=True),
                                    jnp.max(s_n, axis=-1, keepdims=True)), sink)
        e_c = jnp.exp(s_c - m)
        e_n = jnp.exp(s_n - m)
        den = (jnp.sum(e_c, axis=-1, keepdims=True) + jnp.sum(e_n, axis=-1, keepdims=True)
               + jnp.exp(sink - m))
        o = _dot((e_c / den).astype(BF16), vc.astype(BF16)) + _dot((e_n / den).astype(BF16),
                                                                    vn8.astype(BF16))
        o_ref[i] = o.astype(o_ref.dtype)
        ko_ref[i, 0:w - nt, :] = kc_ref[i, nt:w, :]
        ko_ref[i, w - nt:w, :] = kn
        vo_ref[i, 0:w - nt, :] = vc_ref[i, nt:w, :]
        vo_ref[i, w - nt:w, :] = vn
        return carry

    lax.fori_loop(0, q_ref.shape[0], body, 0, unroll=True)


def swa_sample(q, k_new, v_new, k_cache, v_cache, sink_rows, qtabs, ktabs, *, batch):
    sb = SEQ_TILE
    nq = q.shape[1]
    const = lambda i: (0, 0)
    seq3 = lambda i: (i, 0, 0)
    return pl.pallas_call(
        _swa_sample_kernel,
        grid=(batch // sb,),
        in_specs=[pl.BlockSpec((sb, nq, KV_WIDTH), seq3),
                  pl.BlockSpec((sb, SUBLANES, KV_WIDTH), seq3),
                  pl.BlockSpec((sb, SUBLANES, KV_WIDTH), seq3),
                  pl.BlockSpec((sb, WINDOW, KV_WIDTH), seq3),
                  pl.BlockSpec((sb, WINDOW, KV_WIDTH), seq3),
                  pl.BlockSpec((nq, LANES), const),
                  pl.BlockSpec((nq, LANES), const),
                  pl.BlockSpec((nq, LANES), const),
                  pl.BlockSpec((nq, LANES), const),
                  pl.BlockSpec((SUBLANES, LANES), const),
                  pl.BlockSpec((SUBLANES, LANES), const),
                  pl.BlockSpec((SUBLANES, LANES), const)],
        out_specs=[pl.BlockSpec((sb, nq, KV_WIDTH), seq3),
                   pl.BlockSpec((sb, WINDOW, KV_WIDTH), seq3),
                   pl.BlockSpec((sb, WINDOW, KV_WIDTH), seq3)],
        out_shape=[jax.ShapeDtypeStruct((batch, nq, KV_WIDTH), F32),
                   jax.ShapeDtypeStruct((batch, WINDOW, KV_WIDTH), F32),
                   jax.ShapeDtypeStruct((batch, WINDOW, KV_WIDTH), F32)],
        compiler_params=_params("parallel"),
        name="swa_sample",
    )(q, k_new, v_new, k_cache, v_cache, sink_rows, *qtabs, *ktabs)


def _xattn_sample_kernel(q_ref, mk_ref, mv_ref, o_ref):
    rows = X_HEADS * SUBLANES
    row_head = jnp.right_shift(lax.broadcasted_iota(jnp.int32, (rows, N_MEM * X_HEADS), 0), 3)
    col_head = jnp.bitwise_and(lax.broadcasted_iota(jnp.int32, (rows, N_MEM * X_HEADS), 1),
                               X_HEADS - 1)
    own_head = row_head == col_head

    def body(i, carry):
        q8 = q_ref[i]
        qh = jnp.concatenate([q8[:, h * X_HEAD_DIM:(h + 1) * X_HEAD_DIM] for h in range(X_HEADS)],
                             axis=0).astype(BF16)
        s = _dot_nt(qh, mk_ref[i].astype(BF16)) * (X_HEAD_DIM ** -0.5)
        p = _softmax(jnp.where(own_head, s, NEG_INF))
        o_ref[i] = _dot(p.astype(BF16), mv_ref[i].astype(BF16))
        return carry

    lax.fori_loop(0, q_ref.shape[0], body, 0, unroll=True)


def xattn_sample(q, mk, mv, *, batch):
    sb = SEQ_TILE
    seq3 = lambda i: (i, 0, 0)
    rows = X_HEADS * SUBLANES
    return pl.pallas_call(
        _xattn_sample_kernel,
        grid=(batch // sb,),
        in_specs=[pl.BlockSpec((sb, SUBLANES, X_WIDTH), seq3),
                  pl.BlockSpec((sb, N_MEM * X_HEADS, X_HEAD_DIM), seq3),
                  pl.BlockSpec((sb, N_MEM * X_HEADS, X_HEAD_DIM), seq3)],
        out_specs=pl.BlockSpec((sb, rows, X_HEAD_DIM), seq3),
        out_shape=jax.ShapeDtypeStruct((batch, rows, X_HEAD_DIM), F32),
        compiler_params=_params("parallel"),
        name="xattn_sample",
    )(q, mk, mv)


def _matmul_residual_kernel(a_ref, w_ref, x_ref, o_ref):
    o_ref[...] = x_ref[...] + _dot(a_ref[...], w_ref[...])


def matmul_residual(a, w, x, *, tm):
    m, k = a.shape
    n = w.shape[1]
    return pl.pallas_call(
        _matmul_residual_kernel,
        grid=(m // tm,),
        in_specs=[pl.BlockSpec((tm, k), lambda i: (i, 0)),
                  pl.BlockSpec((k, n), lambda i: (0, 0)),
                  pl.BlockSpec((tm, n), lambda i: (i, 0))],
        out_specs=pl.BlockSpec((tm, n), lambda i: (i, 0)),
        out_shape=jax.ShapeDtypeStruct((m, n), F32),
        compiler_params=_params("parallel"),
        name="matmul_residual",
    )(a, w, x)


def _rope_tables(pos, reps):
    half = ROT_DIM // 2
    inv = ROPE_THETA ** (-np.arange(half, dtype=np.float64) * (2.0 / ROT_DIM))
    ang = np.asarray(pos, np.float64)[:, None] * inv[None, :]
    cos, sin = np.cos(ang), np.sin(ang)
    n = ang.shape[0]
    ones = np.ones((n, HEAD_DIM - ROT_DIM))
    zeros = np.zeros((n, HEAD_DIM - ROT_DIM))
    z8 = np.zeros((n, half))
    c = np.concatenate([cos, cos, ones], axis=1)
    sa = np.concatenate([-sin, z8, zeros], axis=1)
    sb = np.concatenate([z8, sin, zeros], axis=1)
    return tuple(np.repeat(np.tile(t, (1, LANES // HEAD_DIM)), reps, axis=0).astype(np.float32)
                 for t in (c, sa, sb))


def _split_w_in(w_in):
    s2 = SSM_WIDTH + CONV_DIM
    s3 = s2 + SSM_HEADS
    assert s2 == COL_Q and COL_Q + (w_in.shape[1] - s3) == COL_DT
    w_dt = jnp.pad(w_in[:, s2:s3], ((0, 0), (0, LANES - SSM_HEADS)))
    return (w_in[:, :s2].astype(BF16), w_in[:, s3:].astype(BF16), w_dt.astype(BF16))


def _pad_lanes(v, width=LANES):
    return jnp.pad(v, (0, width - v.shape[0]))[None, :]


def kernel(x_prompt, x_sample, mem_prompt, state_ssm, state_conv, cache_swa_k, cache_swa_v,
           cache_mem_k, cache_mem_v, norm_mix, w_in, conv_w, conv_b, dt_bias, a_log, d_skip,
           gate_norm, sinks, w_out, norm_mem, norm_x, w_xq, w_xk, w_xv, w_xo, norm_ffn,
           w_gate, w_up, w_down, norm_final):
    bp, tp, _ = x_prompt.shape
    bs, ts, _ = x_sample.shape
    assert ts == DEC_T and tp % CHUNK == 0 and bs % SEQ_TILE == 0

    w_in_p = _split_w_in(w_in[0])
    row = lambda v: v.reshape(1, -1)
    cw = jnp.concatenate([conv_w[0][:, :SSM_WIDTH], conv_w[0][:, SSM_WIDTH:]], axis=1)
    cbias = row(conv_b[0])
    dtb = _pad_lanes(dt_bias[0])
    alog = _pad_lanes(a_log[0])
    dskip_e = row(jnp.repeat(d_skip[0], SSM_HEADDIM))
    dtb_e = row(jnp.repeat(dt_bias[0], SSM_HEADDIM))
    alog_e = row(jnp.repeat(a_log[0], SSM_HEADDIM))
    gn = row(gate_norm[0])
    tril = jnp.asarray(np.tril(np.ones((CHUNK, CHUNK), np.float32)), BF16)
    expand = jnp.asarray(np.arange(LANES)[:, None] == (np.arange(SSM_WIDTH)[None, :] // SSM_HEADDIM),
                         BF16)

    xp = x_prompt.reshape(bp * tp, D_MODEL)
    proj_p = norm_matmul(xp, row(norm_mix[0]), w_in_p, tm=512, tn=512)
    y_p, p_ssm, (w_gate_b, w_up_b, w_down_b) = ssd_prompt(
        proj_p, cw, cbias, dtb, alog, dskip_e, gn, tril, expand, (w_gate[0], w_up[0], w_down[0]),
        batch=bp, seq=tp)
    cos_p, sa_p, sb_p = _rope_tables(np.arange(tp), 1)
    att_p, p_k, (w_out_b, w_xq_b, w_xo_b, w_xk_b, w_xv_b) = swa_prompt(
        proj_p, sinks[0], cos_p, sa_p, sb_p, (w_out[0], w_xq[0], w_xo[0], w_xk[0], w_xv[0]),
        batch=bp, seq=tp)
    w_xkv_b = (w_xk_b, w_xv_b)
    x1_p, qx_p = out_proj(y_p, att_p, xp, w_out_b, row(norm_x[0]), w_xq_b, tm=512)
    mkv = norm_matmul(mem_prompt.reshape(bp * N_MEM, D_MODEL), row(norm_mem[0]), w_xkv_b,
                      tm=bp * N_MEM, tn=512)
    x2_p = xattn_prompt(qx_p, mkv, x1_p, w_xo_b, batch=bp, seq=tp, tq=512)
    y_prompt = ffn(x2_p, row(norm_ffn[0]), w_gate_b, w_up_b, w_down_b, row(norm_final),
                   tm=1024, tf=512).reshape(bp, tp, D_MODEL)

    proj_p3 = proj_p.reshape(bp, tp, PROJ_WIDTH)
    tail = proj_p3[:, tp - (CONV_WIDTH - 1):, :]
    p_conv = jnp.concatenate([tail[:, :, COL_X:COL_X + SSM_WIDTH],
                              tail[:, :, COL_BC:COL_BC + BC_WIDTH]], axis=-1)
    p_v = proj_p3[:, tp - WINDOW:, COL_KV + KV_WIDTH:COL_KV + 2 * KV_WIDTH]
    p_mk = mkv[:, :X_WIDTH].reshape(bp, N_MEM, X_HEADS, X_HEAD_DIM)
    p_mv = mkv[:, X_WIDTH:].reshape(bp, N_MEM, X_HEADS, X_HEAD_DIM)

    xs = x_sample.transpose(1, 0, 2).reshape(ts * bs, D_MODEL)
    proj_s = norm_matmul(xs, row(norm_mix[0]), w_in_p, tm=ts * bs, tn=512)
    proj_s3 = proj_s.reshape(ts, bs, PROJ_WIDTH)
    y_s, s_ssm, s_conv = ssd_sample(
        proj_s3, state_conv[0].transpose(1, 0, 2),
        state_ssm[0].reshape(bs, SSM_WIDTH, D_STATE), cw, cbias, dtb_e, alog_e, dskip_e, gn,
        expand, batch=bs)

    rep = ATT_HEADS // KV_HEADS
    q_s = proj_s3[:, :, COL_Q:COL_Q + ATT_WIDTH].reshape(ts, bs, KV_HEADS, rep, HEAD_DIM)
    q_s = q_s.transpose(1, 2, 0, 3, 4)
    zq = jnp.zeros_like(q_s[:, 0])
    q_bd = jnp.stack([jnp.concatenate([q_s[:, 0], zq], axis=-1),
                      jnp.concatenate([zq, q_s[:, 1]], axis=-1)], axis=1)
    q_bd = q_bd.reshape(bs, KV_HEADS * ts * rep, KV_WIDTH)
    kv_s = proj_s3[:, :, COL_KV:COL_KV + 2 * KV_WIDTH].transpose(1, 0, 2)
    kv_s = jnp.pad(kv_s, ((0, 0), (0, SUBLANES - ts), (0, 0)))
    pos_s = PAST_LEN + np.arange(ts)
    ktabs = _rope_tables(PAST_LEN + np.arange(SUBLANES), 1)
    qtabs = tuple(np.tile(t, (KV_HEADS, 1)) for t in _rope_tables(pos_s, rep))
    sink_rows = jnp.broadcast_to(
        jnp.broadcast_to(sinks[0].reshape(KV_HEADS, 1, rep), (KV_HEADS, ts, rep)).reshape(-1, 1),
        (KV_HEADS * ts * rep, LANES))
    o_bd, s_k, s_v = swa_sample(
        q_bd, kv_s[:, :, :KV_WIDTH], kv_s[:, :, KV_WIDTH:],
        cache_swa_k[0].reshape(bs, WINDOW, KV_WIDTH), cache_swa_v[0].reshape(bs, WINDOW, KV_WIDTH),
        sink_rows, qtabs, ktabs, batch=bs)
    o_bd = o_bd.reshape(bs, KV_HEADS, ts, rep, KV_HEADS, HEAD_DIM)
    att_s = jnp.stack([o_bd[:, g, :, :, g, :] for g in range(KV_HEADS)], axis=2)
    att_s = att_s.transpose(1, 0, 2, 3, 4).reshape(ts * bs, ATT_WIDTH).astype(BF16)

    x1_s, qx_s = out_proj(y_s.reshape(ts * bs, SSM_WIDTH), att_s, xs, w_out_b, row(norm_x[0]),
                          w_xq_b, tm=ts * bs)
    qx_s8 = jnp.pad(qx_s.astype(F32).reshape(ts, bs, X_WIDTH).transpose(1, 0, 2),
                    ((0, 0), (0, SUBLANES - ts), (0, 0)))
    o_x = xattn_sample(qx_s8, cache_mem_k.reshape(bs, N_MEM * X_HEADS, X_HEAD_DIM),
                       cache_mem_v.reshape(bs, N_MEM * X_HEADS, X_HEAD_DIM), batch=bs)
    o_x = o_x.reshape(bs, X_HEADS, SUBLANES, X_HEAD_DIM)[:, :, :ts]
    o_x = o_x.transpose(2, 0, 1, 3).reshape(ts * bs, X_WIDTH).astype(BF16)
    x2_s = matmul_residual(o_x, w_xo_b, x1_s, tm=ts * bs)
    y_s_out = ffn(x2_s, row(norm_ffn[0]), w_gate_b, w_up_b, w_down_b, row(norm_final),
                  tm=ts * bs, tf=512)
    y_sample = y_s_out.reshape(ts, bs, D_MODEL).transpose(1, 0, 2)

    return (y_prompt, y_sample,
            p_ssm.reshape(1, bp, SSM_HEADS, SSM_HEADDIM, D_STATE), p_conv[None],
            p_k.reshape(1, bp, WINDOW, KV_HEADS, HEAD_DIM),
            p_v.reshape(1, bp, WINDOW, KV_HEADS, HEAD_DIM),
            p_mk[None], p_mv[None],
            s_ssm.reshape(1, bs, SSM_HEADS, SSM_HEADDIM, D_STATE),
            s_conv.transpose(1, 0, 2)[None],
            s_k.reshape(1, bs, WINDOW, KV_HEADS, HEAD_DIM),
            s_v.reshape(1, bs, WINDOW, KV_HEADS, HEAD_DIM))
```

```python
import functools

import jax
import jax.numpy as jnp
import numpy as np
from jax import lax
from jax.experimental import pallas as pl
from jax.experimental.pallas import tpu as pltpu

F32 = jnp.float32
BF16 = jnp.bfloat16

D_MODEL = 2048
SSM_WIDTH = 1024
SSM_HEADDIM = 64
SSM_HEADS = 16
SSM_GROUPS = 2
D_STATE = 128
CONV_WIDTH = 4
CONV_DIM = SSM_WIDTH + 2 * SSM_GROUPS * D_STATE
BC_WIDTH = 2 * SSM_GROUPS * D_STATE
ATT_WIDTH = 1024
HEAD_DIM = 64
ATT_HEADS = 16
KV_HEADS = 2
KV_WIDTH = KV_HEADS * HEAD_DIM
WINDOW = 128
CHUNK = 128
ROT_DIM = 16
ROPE_THETA = 500000.0
PAST_LEN = 16384
N_MEM = 256
X_HEADS = 4
X_HEAD_DIM = 128
X_WIDTH = X_HEADS * X_HEAD_DIM
EPS = 1e-5

LANES = 128
SUBLANES = 8
VMEM_LIMIT_BYTES = 56 * 1024 * 1024

PROJ_WIDTH = 3968
COL_Z = 0
COL_X = 1024
COL_BC = 2048
COL_Q = 2560
COL_KV = 3584
COL_DT = 3840
Q_BLOCK = 512

NEG_INF = float("-inf")


def _params(*semantics):
    return pltpu.CompilerParams(dimension_semantics=semantics,
                                vmem_limit_bytes=VMEM_LIMIT_BYTES)


def _rmsnorm(x, gain):
    ms = jnp.mean(x * x, axis=-1, keepdims=True)
    return x * lax.rsqrt(ms + EPS) * gain


def _silu(x):
    return x * jax.nn.sigmoid(x)


def _softplus(x):
    return jnp.maximum(x, 0.0) + jnp.log1p(jnp.exp(-jnp.abs(x)))


def _split_bf16(v, parts):
    out = []
    rem = v
    for _ in range(parts - 1):
        p = rem.astype(BF16)
        out.append(p)
        rem = rem - p.astype(F32)
    out.append(rem.astype(BF16))
    return out


def _dot(a, b):
    return jnp.dot(a, b, preferred_element_type=F32)


def _dot_nt(a, b):
    return lax.dot_general(a, b, (((1,), (1,)), ((), ())), preferred_element_type=F32)


def _dot_tn(a, b):
    return lax.dot_general(a, b, (((0,), (0,)), ((), ())), preferred_element_type=F32)


def _select_matmul(v, sel, parts):
    acc = None
    for p in _split_bf16(v, parts):
        t = _dot(p, sel)
        acc = t if acc is None else acc + t
    return acc


def _norm_matmul_kernel(x_ref, g_ref, *refs, tn):
    w_refs, o_ref = refs[:-1], refs[-1]
    h = _rmsnorm(x_ref[...], g_ref[...]).astype(BF16)
    col = 0
    for w_ref in w_refs:
        width = w_ref.shape[1]
        for j in range(0, width, tn):
            n = min(tn, width - j)
            o_ref[:, col + j:col + j + n] = _dot(h, w_ref[:, j:j + n]).astype(o_ref.dtype)
        col += width


def norm_matmul(x, gain, ws, *, tm, tn, out_dtype=F32):
    m, k = x.shape
    n = sum(w.shape[1] for w in ws)
    return pl.pallas_call(
        functools.partial(_norm_matmul_kernel, tn=tn),
        grid=(m // tm,),
        in_specs=[pl.BlockSpec((tm, k), lambda i: (i, 0)),
                  pl.BlockSpec((1, k), lambda i: (0, 0))]
                 + [pl.BlockSpec(w.shape, lambda i: (0, 0), pipeline_mode=pl.Buffered(1))
                    for w in ws],
        out_specs=pl.BlockSpec((tm, n), lambda i: (i, 0)),
        out_shape=jax.ShapeDtypeStruct((m, n), out_dtype),
        compiler_params=_params("parallel"),
        name="norm_matmul",
    )(x, gain, *ws)


W_IN_ROWS = 256


def _in_proj_cast_kernel(x_ref, g_ref, w_ref, o_ref, wa_ref, wb_ref, wdt_ref, h_ref):
    k = pl.program_id(0)
    kb = w_ref.shape[0]
    s2 = COL_Q
    s3 = s2 + SSM_HEADS

    @pl.when(k == 0)
    def _():
        h = _rmsnorm(x_ref[...], g_ref[...]).astype(BF16)
        for j in range(h_ref.shape[0]):
            h_ref[j] = h[:, j * kb:(j + 1) * kb]
        o_ref[...] = jnp.zeros_like(o_ref)

    w = w_ref[...]
    wa = w[:, 0:s2].astype(BF16)
    wb = w[:, s3:].astype(BF16)
    wdt = jnp.concatenate([w[:, s2:s3], jnp.zeros((kb, LANES - SSM_HEADS), F32)],
                          axis=1).astype(BF16)
    wa_ref[...] = wa
    wb_ref[...] = wb
    wdt_ref[...] = wdt
    hk = h_ref[k]
    o_ref[:, 0:COL_Q] += _dot(hk, wa)
    o_ref[:, COL_Q:COL_DT] += _dot(hk, wb)
    o_ref[:, COL_DT:PROJ_WIDTH] += _dot(hk, wdt)


def in_proj_cast(x, gain, w_in):
    m, k = x.shape
    n_in = w_in.shape[1]
    kb = W_IN_ROWS
    assert k % kb == 0 and COL_DT - COL_Q == n_in - (COL_Q + SSM_HEADS)
    widths = (COL_Q, COL_DT - COL_Q, LANES)
    const = lambda i: (0, 0)
    return pl.pallas_call(
        _in_proj_cast_kernel,
        grid=(k // kb,),
        in_specs=[pl.BlockSpec((m, k), const),
                  pl.BlockSpec((1, k), const),
                  pl.BlockSpec((kb, n_in), lambda i: (i, 0))],
        out_specs=[pl.BlockSpec((m, PROJ_WIDTH), const)]
                  + [pl.BlockSpec((kb, wd), lambda i: (i, 0)) for wd in widths],
        out_shape=[jax.ShapeDtypeStruct((m, PROJ_WIDTH), F32)]
                  + [jax.ShapeDtypeStruct((k, wd), BF16) for wd in widths],
        scratch_shapes=[pltpu.VMEM((k // kb, m, kb), BF16)],
        compiler_params=_params("arbitrary"),
        name="in_proj_cast",
    )(x, gain, w_in)


BF16_SUBLANES = 16


def _with_weight_casts(body, n_in, n_out, n_cast):
    def kernel(*refs):
        ins = refs[:n_in]
        cast_in = refs[n_in:n_in + n_cast]
        outs = refs[n_in + n_cast:n_in + n_cast + n_out]
        cast_out = refs[n_in + n_cast + n_out:n_in + 2 * n_cast + n_out]
        scratch = refs[n_in + 2 * n_cast + n_out:]
        for src, dst in zip(cast_in, cast_out):
            dst[...] = src[...].astype(dst.dtype)
        body(*ins, *outs, *scratch)
    return kernel


def _cast_specs(ws, batch, steps):
    specs, shapes = [], []
    for w in ws:
        r, c = w.shape
        assert r % (steps * BF16_SUBLANES) == 0 and c % (batch * LANES) == 0, w.shape
        specs.append(pl.BlockSpec((r // steps, c // batch), lambda b, i: (i, b)))
        shapes.append(jax.ShapeDtypeStruct(w.shape, BF16))
    return specs, shapes


def _ssd_prompt_kernel(z_ref, x_ref, bc_ref, dt_ref, cw_ref, cb_ref, dtb_ref, alog_ref,
                       dskip_ref, gn_ref, tril_ref, exp_ref,
                       y_ref, state_ref, carry_ref, st_ref):
    c = pl.program_id(1)
    nc = pl.num_programs(1)
    t = CHUNK

    @pl.when(c == 0)
    def _():
        carry_ref[...] = jnp.zeros_like(carry_ref)
        st_ref[...] = jnp.zeros_like(st_ref)

    x_new = jnp.concatenate([x_ref[...], bc_ref[...]], axis=1)
    cat = jnp.concatenate([carry_ref[...], x_new], axis=0)
    acc = jnp.broadcast_to(cb_ref[...], (t, CONV_DIM))
    for k in range(CONV_WIDTH):
        back = CONV_WIDTH - 1 - k
        tap = x_new if back == 0 else pltpu.roll(cat, back, axis=0)[SUBLANES:SUBLANES + t]
        acc = acc + tap * cw_ref[k:k + 1, :]
    xbc = _silu(acc)
    carry_ref[...] = x_new[t - SUBLANES:t]

    xs = xbc[:, 0:SSM_WIDTH]
    xs_b = xs.astype(BF16)
    b_mat = [xbc[:, SSM_WIDTH + g * D_STATE:SSM_WIDTH + (g + 1) * D_STATE].astype(BF16)
             for g in range(SSM_GROUPS)]
    c_off = SSM_WIDTH + SSM_GROUPS * D_STATE
    c_mat = [xbc[:, c_off + g * D_STATE:c_off + (g + 1) * D_STATE].astype(BF16)
             for g in range(SSM_GROUPS)]

    dt = _softplus(dt_ref[...] + dtb_ref[...])
    a = -jnp.exp(alog_ref[...])
    da = dt * a
    cs = _select_matmul_left(tril_ref[...], da)
    cs_t = cs.T
    dt_t = dt.T
    cs_last = cs[t - 1:t, :]
    exp_cs = jnp.exp(cs)
    to_end = jnp.exp(cs_last - cs) * dt
    sel = exp_ref[...]
    exp_cs_e = _select_matmul(exp_cs, sel, 2)
    to_end_e = _select_matmul(to_end, sel, 2)

    row = lax.broadcasted_iota(jnp.int32, (t, 2 * t), 0)
    col = lax.broadcasted_iota(jnp.int32, (t, 2 * t), 1)
    causal = jnp.bitwise_and(col, t - 1) <= row
    lane = lax.broadcasted_iota(jnp.int32, (t, LANES), 1)
    left = lane < SSM_HEADDIM

    cb = [_dot_nt(c_mat[g], b_mat[g]) for g in range(SSM_GROUPS)]
    cb2 = [jnp.concatenate([m, m], axis=1) for m in cb]

    y_parts = []
    for j in range(SSM_HEADS // 2):
        g = (2 * j) // (SSM_HEADS // SSM_GROUPS)
        h0, h1 = 2 * j, 2 * j + 1
        colv = jnp.concatenate([jnp.broadcast_to(cs[:, h0:h0 + 1], (t, t)),
                                jnp.broadcast_to(cs[:, h1:h1 + 1], (t, t))], axis=1)
        rowv = jnp.concatenate([jnp.broadcast_to(cs_t[h0:h0 + 1, :], (t, t)),
                                jnp.broadcast_to(cs_t[h1:h1 + 1, :], (t, t))], axis=1)
        dtr = jnp.concatenate([jnp.broadcast_to(dt_t[h0:h0 + 1, :], (t, t)),
                               jnp.broadcast_to(dt_t[h1:h1 + 1, :], (t, t))], axis=1)
        decay = jnp.where(causal, jnp.exp(colv - rowv), 0.0)
        wts = (cb2[g] * decay * dtr).astype(BF16)
        xp = xs_b[:, j * LANES:(j + 1) * LANES]
        zero = jnp.zeros_like(xp)
        rhs = jnp.concatenate([jnp.where(left, xp, zero), jnp.where(left, zero, xp)], axis=0)
        y_parts.append(_dot(wts, rhs))
    y_diag = jnp.concatenate(y_parts, axis=1)

    half = SSM_WIDTH // SSM_GROUPS
    st = st_ref[...]
    st_b = st.astype(BF16)
    y_off = jnp.concatenate(
        [_dot(c_mat[g], st_b[:, g * half:(g + 1) * half]) for g in range(SSM_GROUPS)], axis=1)
    xw = (xs * to_end_e).astype(BF16)
    upd = jnp.concatenate(
        [_dot_tn(b_mat[g], xw[:, g * half:(g + 1) * half]) for g in range(SSM_GROUPS)], axis=1)
    st_new = st * exp_cs_e[t - 1:t, :] + upd
    st_ref[...] = st_new

    y = y_diag + y_off * exp_cs_e + xs * dskip_ref[...]
    gated = y * _silu(z_ref[...])
    outs = []
    for g in range(SSM_GROUPS):
        outs.append(_rmsnorm(gated[:, g * half:(g + 1) * half], gn_ref[:, g * half:(g + 1) * half]))
    y_ref[...] = jnp.concatenate(outs, axis=1).astype(y_ref.dtype)

    @pl.when(c == nc - 1)
    def _():
        state_ref[0] = st_new.T


def _select_matmul_left(sel, v):
    acc = None
    for p in _split_bf16(v, 3):
        t = _dot(sel, p)
        acc = t if acc is None else acc + t
    return acc


def ssd_prompt(proj, conv_w, conv_b, dtb, alog, dskip_e, gate_norm, tril, expand, casts, *,
               batch, seq):
    nc = seq // CHUNK
    rows = lambda b, c: b * nc + c
    const = lambda b, c: (0, 0)
    in_specs = [
        pl.BlockSpec((CHUNK, SSM_WIDTH), lambda b, c: (rows(b, c), COL_Z // SSM_WIDTH)),
        pl.BlockSpec((CHUNK, SSM_WIDTH), lambda b, c: (rows(b, c), COL_X // SSM_WIDTH)),
        pl.BlockSpec((CHUNK, BC_WIDTH), lambda b, c: (rows(b, c), COL_BC // BC_WIDTH)),
        pl.BlockSpec((CHUNK, LANES), lambda b, c: (rows(b, c), COL_DT // LANES)),
        pl.BlockSpec((CONV_WIDTH, CONV_DIM), const),
        pl.BlockSpec((1, CONV_DIM), const),
        pl.BlockSpec((1, LANES), const),
        pl.BlockSpec((1, LANES), const),
        pl.BlockSpec((1, SSM_WIDTH), const),
        pl.BlockSpec((1, SSM_WIDTH), const),
        pl.BlockSpec((CHUNK, CHUNK), const),
        pl.BlockSpec((LANES, SSM_WIDTH), const),
    ]
    out_specs = [pl.BlockSpec((CHUNK, SSM_WIDTH), lambda b, c: (rows(b, c), 0)),
                 pl.BlockSpec((1, SSM_WIDTH, D_STATE), lambda b, c: (b, 0, 0))]
    out_shape = [jax.ShapeDtypeStruct((batch * seq, SSM_WIDTH), BF16),
                 jax.ShapeDtypeStruct((batch, SSM_WIDTH, D_STATE), F32)]
    cast_specs, cast_shapes = _cast_specs(casts, batch, nc)
    y, state, *cast_out = pl.pallas_call(
        _with_weight_casts(_ssd_prompt_kernel, len(in_specs), len(out_specs), len(casts)),
        grid=(batch, nc),
        in_specs=in_specs + cast_specs,
        out_specs=out_specs + cast_specs,
        out_shape=out_shape + cast_shapes,
        scratch_shapes=[pltpu.VMEM((SUBLANES, CONV_DIM), F32),
                        pltpu.VMEM((D_STATE, SSM_WIDTH), F32)],
        compiler_params=_params("parallel", "arbitrary"),
        name="ssd_prompt",
    )(proj, proj, proj, proj, conv_w, conv_b, dtb, alog, dskip_e, gate_norm, tril, expand, *casts)
    return y, state, cast_out


def _rope(x, cos, sin_a, sin_b):
    parts = []
    for i in range(x.shape[1] // LANES):
        xb = x[:, i * LANES:(i + 1) * LANES]
        up = pltpu.roll(xb, LANES - ROT_DIM // 2, axis=1)
        dn = pltpu.roll(xb, ROT_DIM // 2, axis=1)
        parts.append(xb * cos + up * sin_a + dn * sin_b)
    return parts[0] if len(parts) == 1 else jnp.concatenate(parts, axis=1)


def _block_diag_rows(a, a_swapped, first):
    lane = lax.broadcasted_iota(jnp.int32, a.shape, 1)
    left = lane < HEAD_DIM
    zero = jnp.zeros_like(a)
    if first == 0:
        top, bot = jnp.where(left, a, zero), jnp.where(left, zero, a_swapped)
    else:
        top, bot = jnp.where(left, a_swapped, zero), jnp.where(left, zero, a)
    return jnp.concatenate([top, bot], axis=0)


def _swa_prompt_kernel(sink_ref, qlo_ref, qhi_ref, kv_ref, cos_ref, sa_ref, sb_ref,
                       att_ref, k_ref, kprev_ref, vprev_ref, s_ref, p_ref):
    blk = pl.program_id(1)
    w = WINDOW
    pairs = ATT_HEADS // 2

    @pl.when(blk == 0)
    def _():
        kprev_ref[...] = jnp.zeros_like(kprev_ref)
        vprev_ref[...] = jnp.zeros_like(vprev_ref)

    cos, sa, sb = cos_ref[...], sa_ref[...], sb_ref[...]
    k_cur = _rope(kv_ref[:, 0:KV_WIDTH], cos, sa, sb)
    v_cur = kv_ref[:, KV_WIDTH:2 * KV_WIDTH]
    k_ref[0] = k_cur
    q = jnp.concatenate([qlo_ref[...], qhi_ref[...]], axis=1)
    q = (_rope(q, cos, sa, sb) * (HEAD_DIM ** -0.5)).astype(BF16)

    k_cat = jnp.concatenate([kprev_ref[...], k_cur], axis=0)
    v_cat = jnp.concatenate([vprev_ref[...], v_cur], axis=0)
    k_sw = pltpu.roll(k_cat, HEAD_DIM, axis=1)
    v_sw = pltpu.roll(v_cat, HEAD_DIM, axis=1)
    kbd = [_block_diag_rows(k_cat, k_sw, g).astype(BF16) for g in range(KV_HEADS)]
    vbd_t = [_block_diag_rows(v_cat, v_sw, g).T.astype(BF16) for g in range(KV_HEADS)]

    for j in range(pairs):
        g = (2 * j) // (ATT_HEADS // KV_HEADS)
        s_ref[j] = _dot_nt(kbd[g], q[:, j * LANES:(j + 1) * LANES])

    key = lax.broadcasted_iota(jnp.int32, (w, w), 0)
    qry = lax.broadcasted_iota(jnp.int32, (w, w), 1)
    own_valid = key <= qry
    has_prev = blk > 0
    for j in range(pairs):
        for hh in range(2):
            base = hh * 2 * w
            sink = sink_ref[2 * j + hh]
            prev = jnp.where(has_prev, s_ref[j, base:base + w, :], NEG_INF)
            s = jnp.where(own_valid, s_ref[j, base + w:base + 2 * w, :], prev)
            m = jnp.maximum(jnp.max(s, axis=0, keepdims=True), sink)
            e = jnp.exp(s - m)
            den = jnp.sum(e, axis=0, keepdims=True) + jnp.exp(sink - m)
            p = e * (1.0 / den)
            p_ref[j, base:base + w, :] = jnp.where(own_valid, 0.0, p).astype(BF16)
            p_ref[j, base + w:base + 2 * w, :] = jnp.where(own_valid, p, 0.0).astype(BF16)

    for j in range(pairs):
        g = (2 * j) // (ATT_HEADS // KV_HEADS)
        o_t = _dot(vbd_t[g], p_ref[j])
        att_ref[:, j * LANES:(j + 1) * LANES] = o_t.T.astype(att_ref.dtype)

    kprev_ref[...] = k_cur
    vprev_ref[...] = v_cur


def swa_prompt(proj, sinks, cos, sin_a, sin_b, casts, *, batch, seq):
    nb = seq // WINDOW
    rows = lambda b, i: b * nb + i
    in_specs = [
        pl.BlockSpec(memory_space=pltpu.SMEM),
        pl.BlockSpec((WINDOW, Q_BLOCK), lambda b, i: (rows(b, i), COL_Q // Q_BLOCK)),
        pl.BlockSpec((WINDOW, Q_BLOCK), lambda b, i: (rows(b, i), COL_Q // Q_BLOCK + 1)),
        pl.BlockSpec((WINDOW, 2 * KV_WIDTH), lambda b, i: (rows(b, i), COL_KV // (2 * KV_WIDTH))),
        pl.BlockSpec((WINDOW, LANES), lambda b, i: (i, 0)),
        pl.BlockSpec((WINDOW, LANES), lambda b, i: (i, 0)),
        pl.BlockSpec((WINDOW, LANES), lambda b, i: (i, 0)),
    ]
    out_specs = [pl.BlockSpec((WINDOW, ATT_WIDTH), lambda b, i: (rows(b, i), 0)),
                 pl.BlockSpec((1, WINDOW, KV_WIDTH), lambda b, i: (b, 0, 0))]
    out_shape = [jax.ShapeDtypeStruct((batch * seq, ATT_WIDTH), BF16),
                 jax.ShapeDtypeStruct((batch, WINDOW, KV_WIDTH), F32)]
    cast_specs, cast_shapes = _cast_specs(casts, batch, nb)
    att, k_last, *cast_out = pl.pallas_call(
        _with_weight_casts(_swa_prompt_kernel, len(in_specs), len(out_specs), len(casts)),
        grid=(batch, nb),
        in_specs=in_specs + cast_specs,
        out_specs=out_specs + cast_specs,
        out_shape=out_shape + cast_shapes,
        scratch_shapes=[pltpu.VMEM((WINDOW, KV_WIDTH), F32),
                        pltpu.VMEM((WINDOW, KV_WIDTH), F32),
                        pltpu.VMEM((ATT_HEADS // 2, 4 * WINDOW, WINDOW), F32),
                        pltpu.VMEM((ATT_HEADS // 2, 4 * WINDOW, WINDOW), BF16)],
        compiler_params=_params("parallel", "arbitrary"),
        name="swa_prompt",
    )(sinks, proj, proj, proj, cos, sin_a, sin_b, *casts)
    return att, k_last, cast_out


def _out_proj_kernel(y_ref, a_ref, x_ref, wo_ref, gx_ref, wq_ref, x1_ref, q_ref):
    mix = _dot(y_ref[...], wo_ref[0:SSM_WIDTH, :]) + _dot(a_ref[...], wo_ref[SSM_WIDTH:, :])
    x1 = x_ref[...] + mix
    x1_ref[...] = x1
    h = _rmsnorm(x1, gx_ref[...]).astype(BF16)
    q_ref[...] = _dot(h, wq_ref[...]).astype(q_ref.dtype)


def out_proj(y, att, x, w_out, norm_x, w_xq, *, tm):
    m = x.shape[0]
    const = lambda i: (0, 0)
    return pl.pallas_call(
        _out_proj_kernel,
        grid=(m // tm,),
        in_specs=[pl.BlockSpec((tm, SSM_WIDTH), lambda i: (i, 0)),
                  pl.BlockSpec((tm, ATT_WIDTH), lambda i: (i, 0)),
                  pl.BlockSpec((tm, D_MODEL), lambda i: (i, 0)),
                  pl.BlockSpec((D_MODEL, D_MODEL), const),
                  pl.BlockSpec((1, D_MODEL), const),
                  pl.BlockSpec((D_MODEL, X_WIDTH), const)],
        out_specs=[pl.BlockSpec((tm, D_MODEL), lambda i: (i, 0)),
                   pl.BlockSpec((tm, X_WIDTH), lambda i: (i, 0))],
        out_shape=[jax.ShapeDtypeStruct((m, D_MODEL), F32),
                   jax.ShapeDtypeStruct((m, X_WIDTH), BF16)],
        compiler_params=_params("parallel"),
        name="out_proj",
    )(y, att, x, w_out, norm_x, w_xq)


def _softmax(s):
    m = jnp.max(s, axis=-1, keepdims=True)
    e = jnp.exp(s - m)
    return e / jnp.sum(e, axis=-1, keepdims=True)


def _xattn_prompt_kernel(q_ref, mk_ref, mv_ref, x1_ref, wo_ref, x2_ref):
    mk = mk_ref[...].astype(BF16)
    mv = mv_ref[...].astype(BF16)
    outs = []
    for h in range(X_HEADS):
        sl = slice(h * X_HEAD_DIM, (h + 1) * X_HEAD_DIM)
        s = _dot_nt(q_ref[:, sl], mk[:, sl]) * (X_HEAD_DIM ** -0.5)
        outs.append(_dot(_softmax(s).astype(BF16), mv[:, sl]).astype(BF16))
    o = jnp.concatenate(outs, axis=1)
    x2_ref[...] = x1_ref[...] + _dot(o, wo_ref[...])


def xattn_prompt(q, mkv, x1, w_xo, *, batch, seq, tq):
    nq = seq // tq
    return pl.pallas_call(
        _xattn_prompt_kernel,
        grid=(batch, nq),
        in_specs=[pl.BlockSpec((tq, X_WIDTH), lambda b, i: (b * nq + i, 0)),
                  pl.BlockSpec((N_MEM, X_WIDTH), lambda b, i: (b, 0)),
                  pl.BlockSpec((N_MEM, X_WIDTH), lambda b, i: (b, 1)),
                  pl.BlockSpec((tq, D_MODEL), lambda b, i: (b * nq + i, 0)),
                  pl.BlockSpec((X_WIDTH, D_MODEL), lambda b, i: (0, 0))],
        out_specs=pl.BlockSpec((tq, D_MODEL), lambda b, i: (b * nq + i, 0)),
        out_shape=jax.ShapeDtypeStruct(x1.shape, F32),
        compiler_params=_params("parallel", "parallel"),
        name="xattn_prompt",
    )(q, mkv, mkv, x1, w_xo)


def _ffn_kernel(x_ref, gn_ref, wg_ref, wu_ref, wd_ref, gf_ref, o_ref, h_ref):
    f = pl.program_id(1)

    @pl.when(f == 0)
    def _():
        x = x_ref[...]
        h_ref[...] = _rmsnorm(x, gn_ref[...]).astype(BF16)
        o_ref[...] = x

    h = h_ref[...]
    act = (_silu(_dot(h, wg_ref[...])) * _dot(h, wu_ref[...])).astype(BF16)
    o_ref[...] += _dot(act, wd_ref[...])

    @pl.when(f == pl.num_programs(1) - 1)
    def _():
        o_ref[...] = _rmsnorm(o_ref[...], gf_ref[...])


def ffn(x, norm_ffn, w_gate, w_up, w_down, norm_final, *, tm, tf):
    m = x.shape[0]
    d_ff = w_gate.shape[1]
    return pl.pallas_call(
        _ffn_kernel,
        grid=(m // tm, d_ff // tf),
        in_specs=[pl.BlockSpec((tm, D_MODEL), lambda i, f: (i, 0)),
                  pl.BlockSpec((1, D_MODEL), lambda i, f: (0, 0)),
                  pl.BlockSpec((D_MODEL, tf), lambda i, f: (0, f)),
                  pl.BlockSpec((D_MODEL, tf), lambda i, f: (0, f)),
                  pl.BlockSpec((tf, D_MODEL), lambda i, f: (f, 0)),
                  pl.BlockSpec((1, D_MODEL), lambda i, f: (0, 0))],
        out_specs=pl.BlockSpec((tm, D_MODEL), lambda i, f: (i, 0)),
        out_shape=jax.ShapeDtypeStruct((m, D_MODEL), F32),
        scratch_shapes=[pltpu.VMEM((tm, D_MODEL), BF16)],
        compiler_params=_params("parallel", "arbitrary"),
        name="ffn",
    )(x, norm_ffn, w_gate, w_up, w_down, norm_final)


SEQ_TILE = SUBLANES
DEC_T = 4


def _ssd_sample_kernel(z_ref, x_ref, bc_ref, dt_ref, conv0_ref, h0_ref, cw_ref, cb_ref,
                       dtbe_ref, aloge_ref, dskip_ref, gn_ref, exp_ref,
                       y_ref, hnew_ref, conv_ref):
    nt, sb = DEC_T, SEQ_TILE
    half = SSM_WIDTH // SSM_GROUPS

    xin = [conv0_ref[j] for j in range(CONV_WIDTH - 1)]
    xin += [jnp.concatenate([x_ref[u], bc_ref[u]], axis=1) for u in range(nt)]
    for j in range(CONV_WIDTH - 1):
        conv_ref[j] = xin[nt + j]
    xbc = []
    for u in range(nt):
        acc = jnp.broadcast_to(cb_ref[...], (sb, CONV_DIM))
        for k in range(CONV_WIDTH):
            acc = acc + xin[u + k] * cw_ref[k:k + 1, :]
        xbc.append(_silu(acc))
    xs = [v[:, 0:SSM_WIDTH] for v in xbc]
    c_off = SSM_WIDTH + SSM_GROUPS * D_STATE
    b_rows = [v[:, SSM_WIDTH:c_off] for v in xbc]
    c_rows = [v[:, c_off:CONV_DIM] for v in xbc]

    dt_raw = jnp.concatenate([dt_ref[u] for u in range(nt)], axis=0)
    dt_e = _softplus(_select_matmul(dt_raw, exp_ref[...], 3) + dtbe_ref[...])
    da_e = dt_e * -jnp.exp(aloge_ref[...])
    dts = [dt_e[u * sb:(u + 1) * sb] for u in range(nt)]
    cs = []
    for u in range(nt):
        d = da_e[u * sb:(u + 1) * sb]
        cs.append(d if u == 0 else cs[-1] + d)

    lane = lax.broadcasted_iota(jnp.int32, (sb, SSM_WIDTH), 1)
    first_group = lane < half

    def group_bcast(v0, v1):
        return jnp.where(first_group, v0, v1)

    y = []
    for u in range(nt):
        acc = None
        for s in range(u + 1):
            prod = c_rows[u] * b_rows[s]
            cbv = [jnp.sum(prod[:, g * D_STATE:(g + 1) * D_STATE], axis=-1, keepdims=True)
                   for g in range(SSM_GROUPS)]
            coef = group_bcast(cbv[0], cbv[1]) * jnp.exp(cs[u] - cs[s]) * dts[s]
            term = coef * xs[s]
            acc = term if acc is None else acc + term
        y.append(acc)

    c_stack = jnp.concatenate(c_rows, axis=0).astype(BF16)
    to_end = [jnp.exp(cs[nt - 1] - cs[u]) * dts[u] for u in range(nt)]
    xw = [xs[u] * to_end[u] for u in range(nt)]
    dec_parts = [p.astype(F32) for p in _split_bf16(jnp.exp(cs[nt - 1]), 3)]
    pad_rows = jnp.zeros((sb, SSM_WIDTH), F32)
    lhs_t = jnp.concatenate(xw + dec_parts + [pad_rows], axis=0).T.astype(BF16)
    ones = jnp.ones((sb, D_STATE), F32)
    zeros = jnp.zeros((sb, D_STATE), F32)
    rhs = []
    for g in range(SSM_GROUPS):
        bg = [v[:, g * D_STATE:(g + 1) * D_STATE] for v in b_rows]
        left = jnp.concatenate(bg + [zeros] * 4, axis=0)
        right = jnp.concatenate([zeros] * nt + [ones] * 3 + [zeros], axis=0)
        rhs.append(jnp.concatenate([left, right], axis=1))
    krow = jnp.bitwise_and(lax.broadcasted_iota(jnp.int32, (2 * nt * sb, 2 * D_STATE), 0), sb - 1)
    yrow = jnp.bitwise_and(lax.broadcasted_iota(jnp.int32, (nt * sb, SSM_WIDTH), 0), sb - 1)

    y_off = jnp.zeros((nt * sb, SSM_WIDTH), F32)
    for i in range(sb):
        h0 = h0_ref[i]
        h0_b = h0.astype(BF16)
        off = jnp.concatenate(
            [_dot_nt(c_stack[:, g * D_STATE:(g + 1) * D_STATE], h0_b[g * half:(g + 1) * half])
             for g in range(SSM_GROUPS)], axis=1)
        y_off = y_off + jnp.where(yrow == i, off, 0.0)
        new = []
        for g in range(SSM_GROUPS):
            r = jnp.where(krow == i, rhs[g], 0.0).astype(BF16)
            res = _dot(lhs_t[g * half:(g + 1) * half], r)
            new.append(h0[g * half:(g + 1) * half] * res[:, D_STATE:] + res[:, :D_STATE])
        hnew_ref[i] = jnp.concatenate(new, axis=0)

    for u in range(nt):
        yu = y[u] + y_off[u * sb:(u + 1) * sb] * jnp.exp(cs[u]) + xs[u] * dskip_ref[...]
        gated = yu * _silu(z_ref[u])
        outs = [_rmsnorm(gated[:, g * half:(g + 1) * half], gn_ref[:, g * half:(g + 1) * half])
                for g in range(SSM_GROUPS)]
        y_ref[u] = jnp.concatenate(outs, axis=1).astype(y_ref.dtype)


def ssd_sample(proj, conv0, h0, conv_w, conv_b, dtb_e, alog_e, dskip_e, gate_norm, expand, *,
               batch):
    sb = SEQ_TILE
    const = lambda i: (0, 0)
    return pl.pallas_call(
        _ssd_sample_kernel,
        grid=(batch // sb,),
        in_specs=[
            pl.BlockSpec((DEC_T, sb, SSM_WIDTH), lambda i: (0, i, COL_Z // SSM_WIDTH)),
            pl.BlockSpec((DEC_T, sb, SSM_WIDTH), lambda i: (0, i, COL_X // SSM_WIDTH)),
            pl.BlockSpec((DEC_T, sb, BC_WIDTH), lambda i: (0, i, COL_BC // BC_WIDTH)),
            pl.BlockSpec((DEC_T, sb, LANES), lambda i: (0, i, COL_DT // LANES)),
            pl.BlockSpec((CONV_WIDTH - 1, sb, CONV_DIM), lambda i: (0, i, 0)),
            pl.BlockSpec((sb, SSM_WIDTH, D_STATE), lambda i: (i, 0, 0)),
            pl.BlockSpec((CONV_WIDTH, CONV_DIM), const),
            pl.BlockSpec((1, CONV_DIM), const),
            pl.BlockSpec((1, SSM_WIDTH), const),
            pl.BlockSpec((1, SSM_WIDTH), const),
            pl.BlockSpec((1, SSM_WIDTH), const),
            pl.BlockSpec((1, SSM_WIDTH), const),
            pl.BlockSpec((LANES, SSM_WIDTH), const),
        ],
        out_specs=[pl.BlockSpec((DEC_T, sb, SSM_WIDTH), lambda i: (0, i, 0)),
                   pl.BlockSpec((sb, SSM_WIDTH, D_STATE), lambda i: (i, 0, 0)),
                   pl.BlockSpec((CONV_WIDTH - 1, sb, CONV_DIM), lambda i: (0, i, 0))],
        out_shape=[jax.ShapeDtypeStruct((DEC_T, batch, SSM_WIDTH), BF16),
                   jax.ShapeDtypeStruct((batch, SSM_WIDTH, D_STATE), F32),
                   jax.ShapeDtypeStruct((CONV_WIDTH - 1, batch, CONV_DIM), F32)],
        compiler_params=_params("parallel"),
        name="ssd_sample",
    )(proj, proj, proj, proj, conv0, h0, conv_w, conv_b, dtb_e, alog_e, dskip_e, gate_norm, expand)


def _swa_sample_kernel(q_ref, kn_ref, vn_ref, kc_ref, vc_ref, sink_ref, qcos_ref, qsa_ref,
                       qsb_ref, kcos_ref, ksa_ref, ksb_ref, o_ref, ko_ref, vo_ref):
    w = WINDOW
    nt = DEC_T
    nq = KV_HEADS * nt * (ATT_HEADS // KV_HEADS)
    def token_of(rows):
        return jnp.bitwise_and(jnp.right_shift(rows, 3), nt - 1)

    row_t = token_of(lax.broadcasted_iota(jnp.int32, (nq, w), 0))
    col = lax.broadcasted_iota(jnp.int32, (nq, w), 1)
    mask_c = col > row_t
    col_n = lax.broadcasted_iota(jnp.int32, (nq, SUBLANES), 1)
    row_n = token_of(lax.broadcasted_iota(jnp.int32, (nq, SUBLANES), 0))
    mask_n = col_n <= row_n
    sink = sink_ref[...][:, 0:1]

    def body(i, carry):
        q = (_rope(q_ref[i], qcos_ref[...], qsa_ref[...], qsb_ref[...])
             * (HEAD_DIM ** -0.5)).astype(BF16)
        kn8 = _rope(kn_ref[i], kcos_ref[...], ksa_ref[...], ksb_ref[...])
        vn8 = vn_ref[i]
        kn = kn8[0:nt]
        vn = vn8[0:nt]
        kc = kc_ref[i]
        vc = vc_ref[i]
        s_c = jnp.where(mask_c, _dot_nt(q, kc.astype(BF16)), NEG_INF)
        s_n = jnp.where(mask_n, _dot_nt(q, kn8.astype(BF16)), NEG_INF)
        m = jnp.maximum(jnp.maximum(jnp.max(s_c, axis=-1, keepdims=True),
                                    jnp.max(s_n, axis=-1, keepdims=True)), sink)
        e_c = jnp.exp(s_c - m)
        e_n = jnp.exp(s_n - m)
        den = (jnp.sum(e_c, axis=-1, keepdims=True) + jnp.sum(e_n, axis=-1, keepdims=True)
               + jnp.exp(sink - m))
        o = _dot((e_c / den).astype(BF16), vc.astype(BF16)) + _dot((e_n / den).astype(BF16),
                                                                    vn8.astype(BF16))
        o_ref[i] = o.astype(o_ref.dtype)
        ko_ref[i, 0:w - nt, :] = kc_ref[i, nt:w, :]
        ko_ref[i, w - nt:w, :] = kn
        vo_ref[i, 0:w - nt, :] = vc_ref[i, nt:w, :]
        vo_ref[i, w - nt:w, :] = vn
        return carry

    lax.fori_loop(0, q_ref.shape[0], body, 0, unroll=True)


def swa_sample(q, k_new, v_new, k_cache, v_cache, sink_rows, qtabs, ktabs, *, batch):
    sb = SEQ_TILE
    nq = q.shape[1]
    const = lambda i: (0, 0)
    seq3 = lambda i: (i, 0, 0)
    return pl.pallas_call(
        _swa_sample_kernel,
        grid=(batch // sb,),
        in_specs=[pl.BlockSpec((sb, nq, KV_WIDTH), seq3),
                  pl.BlockSpec((sb, SUBLANES, KV_WIDTH), seq3),
                  pl.BlockSpec((sb, SUBLANES, KV_WIDTH), seq3),
                  pl.BlockSpec((sb, WINDOW, KV_WIDTH), seq3),
                  pl.BlockSpec((sb, WINDOW, KV_WIDTH), seq3),
                  pl.BlockSpec((nq, LANES), const),
                  pl.BlockSpec((nq, LANES), const),
                  pl.BlockSpec((nq, LANES), const),
                  pl.BlockSpec((nq, LANES), const),
                  pl.BlockSpec((SUBLANES, LANES), const),
                  pl.BlockSpec((SUBLANES, LANES), const),
                  pl.BlockSpec((SUBLANES, LANES), const)],
        out_specs=[pl.BlockSpec((sb, nq, KV_WIDTH), seq3),
                   pl.BlockSpec((sb, WINDOW, KV_WIDTH), seq3),
                   pl.BlockSpec((sb, WINDOW, KV_WIDTH), seq3)],
        out_shape=[jax.ShapeDtypeStruct((batch, nq, KV_WIDTH), F32),
                   jax.ShapeDtypeStruct((batch, WINDOW, KV_WIDTH), F32),
                   jax.ShapeDtypeStruct((batch, WINDOW, KV_WIDTH), F32)],
        compiler_params=_params("parallel"),
        name="swa_sample",
    )(q, k_new, v_new, k_cache, v_cache, sink_rows, *qtabs, *ktabs)


def _xattn_sample_kernel(q_ref, mk_ref, mv_ref, o_ref):
    rows = X_HEADS * SUBLANES
    row_head = jnp.right_shift(lax.broadcasted_iota(jnp.int32, (rows, N_MEM * X_HEADS), 0), 3)
    col_head = jnp.bitwise_and(lax.broadcasted_iota(jnp.int32, (rows, N_MEM * X_HEADS), 1),
                               X_HEADS - 1)
    own_head = row_head == col_head

    def body(i, carry):
        q8 = q_ref[i]
        qh = jnp.concatenate([q8[:, h * X_HEAD_DIM:(h + 1) * X_HEAD_DIM] for h in range(X_HEADS)],
                             axis=0).astype(BF16)
        s = _dot_nt(qh, mk_ref[i].astype(BF16)) * (X_HEAD_DIM ** -0.5)
        p = _softmax(jnp.where(own_head, s, NEG_INF))
        o_ref[i] = _dot(p.astype(BF16), mv_ref[i].astype(BF16))
        return carry

    lax.fori_loop(0, q_ref.shape[0], body, 0, unroll=True)


def xattn_sample(q, mk, mv, *, batch):
    sb = SEQ_TILE
    seq3 = lambda i: (i, 0, 0)
    rows = X_HEADS * SUBLANES
    return pl.pallas_call(
        _xattn_sample_kernel,
        grid=(batch // sb,),
        in_specs=[pl.BlockSpec((sb, SUBLANES, X_WIDTH), seq3),
                  pl.BlockSpec((sb, N_MEM * X_HEADS, X_HEAD_DIM), seq3),
                  pl.BlockSpec((sb, N_MEM * X_HEADS, X_HEAD_DIM), seq3)],
        out_specs=pl.BlockSpec((sb, rows, X_HEAD_DIM), seq3),
        out_shape=jax.ShapeDtypeStruct((batch, rows, X_HEAD_DIM), F32),
        compiler_params=_params("parallel"),
        name="xattn_sample",
    )(q, mk, mv)


def _matmul_residual_kernel(a_ref, w_ref, x_ref, o_ref):
    o_ref[...] = x_ref[...] + _dot(a_ref[...], w_ref[...])


def matmul_residual(a, w, x, *, tm):
    m, k = a.shape
    n = w.shape[1]
    return pl.pallas_call(
        _matmul_residual_kernel,
        grid=(m // tm,),
        in_specs=[pl.BlockSpec((tm, k), lambda i: (i, 0)),
                  pl.BlockSpec((k, n), lambda i: (0, 0)),
                  pl.BlockSpec((tm, n), lambda i: (i, 0))],
        out_specs=pl.BlockSpec((tm, n), lambda i: (i, 0)),
        out_shape=jax.ShapeDtypeStruct((m, n), F32),
        compiler_params=_params("parallel"),
        name="matmul_residual",
    )(a, w, x)


def _rope_tables(pos, reps):
    half = ROT_DIM // 2
    inv = ROPE_THETA ** (-np.arange(half, dtype=np.float64) * (2.0 / ROT_DIM))
    ang = np.asarray(pos, np.float64)[:, None] * inv[None, :]
    cos, sin = np.cos(ang), np.sin(ang)
    n = ang.shape[0]
    ones = np.ones((n, HEAD_DIM - ROT_DIM))
    zeros = np.zeros((n, HEAD_DIM - ROT_DIM))
    z8 = np.zeros((n, half))
    c = np.concatenate([cos, cos, ones], axis=1)
    sa = np.concatenate([-sin, z8, zeros], axis=1)
    sb = np.concatenate([z8, sin, zeros], axis=1)
    return tuple(np.repeat(np.tile(t, (1, LANES // HEAD_DIM)), reps, axis=0).astype(np.float32)
                 for t in (c, sa, sb))


def _pad_lanes(v, width=LANES):
    return jnp.pad(v, (0, width - v.shape[0]))[None, :]


def kernel(x_prompt, x_sample, mem_prompt, state_ssm, state_conv, cache_swa_k, cache_swa_v,
           cache_mem_k, cache_mem_v, norm_mix, w_in, conv_w, conv_b, dt_bias, a_log, d_skip,
           gate_norm, sinks, w_out, norm_mem, norm_x, w_xq, w_xk, w_xv, w_xo, norm_ffn,
           w_gate, w_up, w_down, norm_final):
    bp, tp, _ = x_prompt.shape
    bs, ts, _ = x_sample.shape
    assert ts == DEC_T and tp % CHUNK == 0 and bs % SEQ_TILE == 0

    row = lambda v: v.reshape(1, -1)
    cw = jnp.concatenate([conv_w[0][:, :SSM_WIDTH], conv_w[0][:, SSM_WIDTH:]], axis=1)
    cbias = row(conv_b[0])
    dtb = _pad_lanes(dt_bias[0])
    alog = _pad_lanes(a_log[0])
    dskip_e = row(jnp.repeat(d_skip[0], SSM_HEADDIM))
    dtb_e = row(jnp.repeat(dt_bias[0], SSM_HEADDIM))
    alog_e = row(jnp.repeat(a_log[0], SSM_HEADDIM))
    gn = row(gate_norm[0])
    tril = jnp.asarray(np.tril(np.ones((CHUNK, CHUNK), np.float32)), BF16)
    expand = jnp.asarray(np.arange(LANES)[:, None] == (np.arange(SSM_WIDTH)[None, :] // SSM_HEADDIM),
                         BF16)

    xs = x_sample.transpose(1, 0, 2).reshape(ts * bs, D_MODEL)
    proj_s, *w_in_p = in_proj_cast(xs, row(norm_mix[0]), w_in[0])

    xp = x_prompt.reshape(bp * tp, D_MODEL)
    proj_p = norm_matmul(xp, row(norm_mix[0]), w_in_p, tm=512, tn=512)
    y_p, p_ssm, (w_gate_b, w_up_b) = ssd_prompt(
        proj_p, cw, cbias, dtb, alog, dskip_e, gn, tril, expand, (w_gate[0], w_up[0]),
        batch=bp, seq=tp)
    cos_p, sa_p, sb_p = _rope_tables(np.arange(tp), 1)
    att_p, p_k, (w_down_b, w_out_b, w_xq_b, w_xo_b, w_xk_b, w_xv_b) = swa_prompt(
        proj_p, sinks[0], cos_p, sa_p, sb_p,
        (w_down[0], w_out[0], w_xq[0], w_xo[0], w_xk[0], w_xv[0]), batch=bp, seq=tp)
    w_xkv_b = (w_xk_b, w_xv_b)
    x1_p, qx_p = out_proj(y_p, att_p, xp, w_out_b, row(norm_x[0]), w_xq_b, tm=512)
    mkv = norm_matmul(mem_prompt.reshape(bp * N_MEM, D_MODEL), row(norm_mem[0]), w_xkv_b,
                      tm=bp * N_MEM, tn=512)
    x2_p = xattn_prompt(qx_p, mkv, x1_p, w_xo_b, batch=bp, seq=tp, tq=512)
    y_prompt = ffn(x2_p, row(norm_ffn[0]), w_gate_b, w_up_b, w_down_b, row(norm_final),
                   tm=1024, tf=512).reshape(bp, tp, D_MODEL)

    proj_p3 = proj_p.reshape(bp, tp, PROJ_WIDTH)
    tail = proj_p3[:, tp - (CONV_WIDTH - 1):, :]
    p_conv = jnp.concatenate([tail[:, :, COL_X:COL_X + SSM_WIDTH],
                              tail[:, :, COL_BC:COL_BC + BC_WIDTH]], axis=-1)
    p_v = proj_p3[:, tp - WINDOW:, COL_KV + KV_WIDTH:COL_KV + 2 * KV_WIDTH]
    p_mk = mkv[:, :X_WIDTH].reshape(bp, N_MEM, X_HEADS, X_HEAD_DIM)
    p_mv = mkv[:, X_WIDTH:].reshape(bp, N_MEM, X_HEADS, X_HEAD_DIM)

    proj_s3 = proj_s.reshape(ts, bs, PROJ_WIDTH)
    y_s, s_ssm, s_conv = ssd_sample(
        proj_s3, state_conv[0].transpose(1, 0, 2),
        state_ssm[0].reshape(bs, SSM_WIDTH, D_STATE), cw, cbias, dtb_e, alog_e, dskip_e, gn,
        expand, batch=bs)

    rep = ATT_HEADS // KV_HEADS
    q_s = proj_s3[:, :, COL_Q:COL_Q + ATT_WIDTH].reshape(ts, bs, KV_HEADS, rep, HEAD_DIM)
    q_s = q_s.transpose(1, 2, 0, 3, 4)
    zq = jnp.zeros_like(q_s[:, 0])
    q_bd = jnp.stack([jnp.concatenate([q_s[:, 0], zq], axis=-1),
                      jnp.concatenate([zq, q_s[:, 1]], axis=-1)], axis=1)
    q_bd = q_bd.reshape(bs, KV_HEADS * ts * rep, KV_WIDTH)
    kv_s = proj_s3[:, :, COL_KV:COL_KV + 2 * KV_WIDTH].transpose(1, 0, 2)
    kv_s = jnp.pad(kv_s, ((0, 0), (0, SUBLANES - ts), (0, 0)))
    pos_s = PAST_LEN + np.arange(ts)
    ktabs = _rope_tables(PAST_LEN + np.arange(SUBLANES), 1)
    qtabs = tuple(np.tile(t, (KV_HEADS, 1)) for t in _rope_tables(pos_s, rep))
    sink_rows = jnp.broadcast_to(
        jnp.broadcast_to(sinks[0].reshape(KV_HEADS, 1, rep), (KV_HEADS, ts, rep)).reshape(-1, 1),
        (KV_HEADS * ts * rep, LANES))
    o_bd, s_k, s_v = swa_sample(
        q_bd, kv_s[:, :, :KV_WIDTH], kv_s[:, :, KV_WIDTH:],
        cache_swa_k[0].reshape(bs, WINDOW, KV_WIDTH), cache_swa_v[0].reshape(bs, WINDOW, KV_WIDTH),
        sink_rows, qtabs, ktabs, batch=bs)
    o_bd = o_bd.reshape(bs, KV_HEADS, ts, rep, KV_HEADS, HEAD_DIM)
    att_s = jnp.stack([o_bd[:, g, :, :, g, :] for g in range(KV_HEADS)], axis=2)
    att_s = att_s.transpose(1, 0, 2, 3, 4).reshape(ts * bs, ATT_WIDTH).astype(BF16)

    x1_s, qx_s = out_proj(y_s.reshape(ts * bs, SSM_WIDTH), att_s, xs, w_out_b, row(norm_x[0]),
                          w_xq_b, tm=ts * bs)
    qx_s8 = jnp.pad(qx_s.astype(F32).reshape(ts, bs, X_WIDTH).transpose(1, 0, 2),
                    ((0, 0), (0, SUBLANES - ts), (0, 0)))
    o_x = xattn_sample(qx_s8, cache_mem_k.reshape(bs, N_MEM * X_HEADS, X_HEAD_DIM),
                       cache_mem_v.reshape(bs, N_MEM * X_HEADS, X_HEAD_DIM), batch=bs)
    o_x = o_x.reshape(bs, X_HEADS, SUBLANES, X_HEAD_DIM)[:, :, :ts]
    o_x = o_x.transpose(2, 0, 1, 3).reshape(ts * bs, X_WIDTH).astype(BF16)
    x2_s = matmul_residual(o_x, w_xo_b, x1_s, tm=ts * bs)
    y_s_out = ffn(x2_s, row(norm_ffn[0]), w_gate_b, w_up_b, w_down_b, row(norm_final),
                  tm=ts * bs, tf=512)
    y_sample = y_s_out.reshape(ts, bs, D_MODEL).transpose(1, 0, 2)

    return (y_prompt, y_sample,
            p_ssm.reshape(1, bp, SSM_HEADS, SSM_HEADDIM, D_STATE), p_conv[None],
            p_k.reshape(1, bp, WINDOW, KV_HEADS, HEAD_DIM),
            p_v.reshape(1, bp, WINDOW, KV_HEADS, HEAD_DIM),
            p_mk[None], p_mv[None],
            s_ssm.reshape(1, bs, SSM_HEADS, SSM_HEADDIM, D_STATE),
            s_conv.transpose(1, 0, 2)[None],
            s_k.reshape(1, bs, WINDOW, KV_HEADS, HEAD_DIM),
            s_v.reshape(1, bs, WINDOW, KV_HEADS, HEAD_DIM))
```
